```python
import jax, jax.numpy as jnp
from jax import lax
import numpy as np

D_MODEL = 2048
BATCH = 8
SEQ = 2048
DEPTH = 1
DEC_BATCH = 128
DEC_SEQ = 4
PAST_LEN = 16384
PAGE_SIZE = 128

MIX_WIDTH = D_MODEL
N_HEADS = 8
QK_NOPE = 128
QK_ROPE = 64
QK_HEAD = QK_NOPE + QK_ROPE
V_HEAD = 128
ATTN_WIDTH = N_HEADS * V_HEAD
Q_LORA = 512
KV_LORA = 256
POOL_WIDTH = MIX_WIDTH - ATTN_WIDTH
POOL_WINDOWS = (2, 4, 8, 16)
N_POOL_GROUPS = len(POOL_WINDOWS)
POOL_GROUP = POOL_WIDTH // N_POOL_GROUPS
POOL_HIST = max(POOL_WINDOWS) - 1
IN_WIDTH = Q_LORA + KV_LORA + QK_ROPE + POOL_WIDTH
SPLITS = (Q_LORA, Q_LORA + KV_LORA, Q_LORA + KV_LORA + QK_ROPE)
D_FF = ((8 * D_MODEL + 3 * 256 - 1) // (3 * 256)) * 256
ROPE_THETA = 10000.0
EPS = 1e-6
Q_BLOCK = 128
SCALE = QK_HEAD ** -0.5

kernel_name = "hymba_mla_multiscale_pool_adaln_step"


def rms_norm(x, w):
    xf = x.astype(jnp.float32)
    y = xf * lax.rsqrt(jnp.mean(xf * xf, axis=-1, keepdims=True) + EPS)
    return (y * w.astype(jnp.float32)).astype(x.dtype)


def rope_tables(pos):
    inv = 1.0 / (ROPE_THETA ** (jnp.arange(0, QK_ROPE, 2, dtype=jnp.float32) / QK_ROPE))
    ang = pos.astype(jnp.float32)[:, None] * inv[None, :]
    return jnp.cos(ang), jnp.sin(ang)


def apply_rope(x, cos, sin):
    x1, x2 = jnp.split(x.astype(jnp.float32), 2, axis=-1)
    return jnp.concatenate([x1 * cos - x2 * sin, x1 * sin + x2 * cos], axis=-1).astype(x.dtype)


def qk_gain(g_nope, g_rope_half):
    return jnp.concatenate([g_nope, g_rope_half, g_rope_half], axis=-1)


def mixer_inputs(x, c, pos, w_mod, b_mod, norm1, w_in, q_a_norm, w_q_b, kv_a_norm, q_gain):
    B, T, _ = x.shape
    mod = jax.nn.silu(c) @ w_mod + b_mod
    sh1, sc1, g1, sh2, sc2, g2 = jnp.split(mod[:, None, :], 6, axis=-1)
    h = rms_norm(x, norm1) * (1.0 + sc1) + sh1
    z = h @ w_in
    c_q, c_kv, k_r, u = jnp.split(z, SPLITS, axis=-1)
    cos, sin = rope_tables(pos)
    q = (rms_norm(c_q, q_a_norm) @ w_q_b).reshape(B, T, N_HEADS, QK_HEAD)
    q = jnp.concatenate([q[..., :QK_NOPE], apply_rope(q[..., QK_NOPE:], cos[:, None], sin[:, None])], axis=-1)
    q = rms_norm(q, q_gain)
    lat = rms_norm(c_kv, kv_a_norm)
    k_r = apply_rope(k_r, cos, sin)
    return q, lat, k_r, u, (g1, sh2, sc2, g2)


def expand_kv(lat, k_rope, w_kv_b, k_gain):
    kv = (lat @ w_kv_b).reshape(lat.shape[:-1] + (N_HEADS, QK_NOPE + V_HEAD))
    k_nope, v = jnp.split(kv, [QK_NOPE], axis=-1)
    k_r = jnp.broadcast_to(k_rope[..., None, :], k_nope.shape[:-1] + (QK_ROPE,))
    k = rms_norm(jnp.concatenate([k_nope, k_r], axis=-1), k_gain)
    return k, v


def attend(q, k, v, q_pos, k_pos):
    s = jnp.einsum('bqhd,bkhd->bhqk', q, k, preferred_element_type=jnp.float32) * SCALE
    s = jnp.where(k_pos[None, :] <= q_pos[:, None], s, -jnp.inf)
    p = jax.nn.softmax(s, axis=-1).astype(v.dtype)
    return jnp.einsum('bhqk,bkhd->bqhd', p, v)


def prompt_attention(q, k, v):
    B, S = q.shape[0], q.shape[1]
    nb = S // Q_BLOCK
    qb = q.reshape(B, nb, Q_BLOCK, N_HEADS, QK_HEAD).transpose(1, 0, 2, 3, 4)
    k_pos = jnp.arange(S)

    def one_block(args):
        i, qi = args
        return attend(qi, k, v, i * Q_BLOCK + jnp.arange(Q_BLOCK), k_pos)

    o = lax.map(one_block, (jnp.arange(nb), qb))
    return o.transpose(1, 0, 2, 3, 4).reshape(B, S, ATTN_WIDTH)


def sample_attention(q, lat_new, kr_new, cache_lat, cache_kr, page_table, w_kv_b, k_gain):
    Bd, T = q.shape[0], q.shape[1]
    past = page_table.shape[1] * cache_lat.shape[1]
    k_pos = jnp.arange(past + T)
    q_pos = past + jnp.arange(T)

    def one_seq(args):
        pt, qi, ln, kn = args
        lat = jnp.concatenate([cache_lat[pt].reshape(past, KV_LORA), ln], axis=0)
        kr = jnp.concatenate([cache_kr[pt].reshape(past, QK_ROPE), kn], axis=0)
        k, v = expand_kv(lat, kr, w_kv_b, k_gain)
        return attend(qi[None], k[None], v[None], q_pos, k_pos)[0]

    o = lax.map(one_seq, (page_table, q, lat_new, kr_new))
    return o.reshape(Bd, T, ATTN_WIDTH)


def pool_mix(u, hist, first_pos, w_pool, b_pool, pool_scale):
    B, T, _ = u.shape
    full = jnp.concatenate([hist, u], axis=1)
    cs = jnp.cumsum(full.astype(jnp.float32), axis=1)
    cs = jnp.concatenate([jnp.zeros_like(cs[:, :1]), cs], axis=1)
    pos = first_pos + jnp.arange(T)
    end = POOL_HIST + 1
    means = []
    for g, w in enumerate(POOL_WINDOWS):
        ch = slice(g * POOL_GROUP, (g + 1) * POOL_GROUP)
        s = cs[:, end:end + T, ch] - cs[:, end - w:end - w + T, ch]
        cnt = jnp.minimum(pos + 1, w).astype(jnp.float32)
        means.append(s / cnt[None, :, None])
    d = (jnp.concatenate(means, axis=-1) - u.astype(jnp.float32)).reshape(B, T, N_POOL_GROUPS, POOL_GROUP)
    o = jnp.einsum('btgc,gcd->btgd', d, w_pool.astype(jnp.float32)) + b_pool
    o = (o.reshape(B, T, POOL_WIDTH) * pool_scale).astype(u.dtype)
    return o, full[:, -POOL_HIST:]


def layer_out(x, attn_o, pool_o, mods, w_out, norm2, w_gate, w_up, w_down):
    g1, sh2, sc2, g2 = mods
    x = x + g1 * (jnp.concatenate([attn_o, pool_o], axis=-1) @ w_out)
    h = rms_norm(x, norm2) * (1.0 + sc2) + sh2
    f = (jax.nn.silu(h @ w_gate) * (h @ w_up)) @ w_down
    return x + g2 * f


def setup_inputs(seed: int = 0) -> dict:
    key = jax.random.key(seed)
    keys = iter(jax.random.split(key, 40))

    def normal(shape, scale):
        return scale * jax.random.normal(next(keys), shape, jnp.float32)

    def gain(shape):
        return 1.0 + 0.1 * jax.random.normal(next(keys), shape, jnp.float32)

    L = DEPTH
    n_pages = PAST_LEN // PAGE_SIZE
    n_used = DEC_BATCH * n_pages
    n_phys = (n_used * 5) // 4
    page_table = jax.random.permutation(next(keys), n_phys)[:n_used].reshape(DEC_BATCH, n_pages).astype(jnp.int32)
    return {
        'x_prompt': normal((BATCH, SEQ, D_MODEL), 1.0),
        'x_sample': normal((DEC_BATCH, DEC_SEQ, D_MODEL), 1.0),
        'cache_kv_latent': normal((L, n_phys, PAGE_SIZE, KV_LORA), 1.0),
        'cache_k_rope': normal((L, n_phys, PAGE_SIZE, QK_ROPE), 1.0),
        'state_pool': normal((L, DEC_BATCH, POOL_HIST, POOL_WIDTH), 1.0),
        'page_table': page_table,
        'c_prompt': normal((BATCH, D_MODEL), 1.0),
        'c_sample': normal((DEC_BATCH, D_MODEL), 1.0),
        'w_mod': normal((L, D_MODEL, 6 * D_MODEL), 0.5 * D_MODEL ** -0.5),
        'b_mod': normal((L, 6 * D_MODEL), 0.01),
        'norm1': gain((L, D_MODEL)),
        'w_in': normal((L, D_MODEL, IN_WIDTH), D_MODEL ** -0.5),
        'q_a_norm': gain((L, Q_LORA)),
        'w_q_b': normal((L, Q_LORA, N_HEADS * QK_HEAD), Q_LORA ** -0.5),
        'kv_a_norm': gain((L, KV_LORA)),
        'w_kv_b': normal((L, KV_LORA, N_HEADS * (QK_NOPE + V_HEAD)), KV_LORA ** -0.5),
        'q_norm_nope': gain((L, QK_NOPE)),
        'q_norm_rope': gain((L, QK_ROPE // 2)),
        'k_norm_nope': gain((L, QK_NOPE)),
        'k_norm_rope': gain((L, QK_ROPE // 2)),
        'w_pool': normal((L, N_POOL_GROUPS, POOL_GROUP, POOL_GROUP), POOL_GROUP ** -0.5),
        'b_pool': normal((L, N_POOL_GROUPS, POOL_GROUP), 0.01),
        'pool_scale': gain((L, POOL_WIDTH)),
        'w_out': normal((L, MIX_WIDTH, D_MODEL), MIX_WIDTH ** -0.5),
        'norm2': gain((L, D_MODEL)),
        'w_gate': normal((L, D_MODEL, D_FF), D_MODEL ** -0.5),
        'w_up': normal((L, D_MODEL, D_FF), D_MODEL ** -0.5),
        'w_down': normal((L, D_FF, D_MODEL), D_FF ** -0.5),
    }


def reference(x_prompt, x_sample, cache_kv_latent, cache_k_rope, state_pool, page_table, c_prompt, c_sample,
              w_mod, b_mod, norm1, w_in, q_a_norm, w_q_b, kv_a_norm, w_kv_b,
              q_norm_nope, q_norm_rope, k_norm_nope, k_norm_rope,
              w_pool, b_pool, pool_scale, w_out, norm2, w_gate, w_up, w_down):
    S = x_prompt.shape[1]
    T = x_sample.shape[1]
    past = page_table.shape[1] * cache_kv_latent.shape[2]
    pos_p = jnp.arange(S)
    pos_s = past + jnp.arange(T)
    yp, ys = x_prompt, x_sample
    lat_p_all, kr_p_all, pool_p_all = [], [], []
    lat_s_all, kr_s_all, pool_s_all = [], [], []
    for l in range(DEPTH):
        q_gain = qk_gain(q_norm_nope[l], q_norm_rope[l])
        k_gain = qk_gain(k_norm_nope[l], k_norm_rope[l])
        qp, latp, krp, up, modp = mixer_inputs(yp, c_prompt, pos_p, w_mod[l], b_mod[l], norm1[l], w_in[l],
                                               q_a_norm[l], w_q_b[l], kv_a_norm[l], q_gain)
        kp, vp = expand_kv(latp, krp, w_kv_b[l], k_gain)
        ap = prompt_attention(qp, kp, vp)
        hist0 = jnp.zeros((up.shape[0], POOL_HIST, POOL_WIDTH), up.dtype)
        pp, hp = pool_mix(up, hist0, 0, w_pool[l], b_pool[l], pool_scale[l])
        yp = layer_out(yp, ap, pp, modp, w_out[l], norm2[l], w_gate[l], w_up[l], w_down[l])
        qs, lats, krs, us, mods = mixer_inputs(ys, c_sample, pos_s, w_mod[l], b_mod[l], norm1[l], w_in[l],
                                               q_a_norm[l], w_q_b[l], kv_a_norm[l], q_gain)
        a_s = sample_attention(qs, lats, krs, cache_kv_latent[l], cache_k_rope[l], page_table, w_kv_b[l], k_gain)
        ps, hs = pool_mix(us, state_pool[l], past, w_pool[l], b_pool[l], pool_scale[l])
        ys = layer_out(ys, a_s, ps, mods, w_out[l], norm2[l], w_gate[l], w_up[l], w_down[l])
        lat_p_all.append(latp); kr_p_all.append(krp); pool_p_all.append(hp)
        lat_s_all.append(lats); kr_s_all.append(krs); pool_s_all.append(hs)
    new_lat_p = jnp.stack(lat_p_all)
    new_kr_p = jnp.stack(kr_p_all)
    new_pool_p = jnp.stack(pool_p_all)
    new_lat_s = jnp.stack(lat_s_all)
    new_kr_s = jnp.stack(kr_s_all)
    new_pool_s = jnp.stack(pool_s_all)
    return (yp, ys, new_lat_p, new_kr_p, new_pool_p, new_lat_s, new_kr_s, new_pool_s)
```

```python
import functools

import jax
import jax.numpy as jnp
from jax import lax
from jax.experimental import pallas as pl
from jax.experimental.pallas import tpu as pltpu

F32 = jnp.float32
BF16 = jnp.bfloat16

N_HEADS = 8
QK_NOPE = 128
QK_ROPE = 64
QK_HEAD = QK_NOPE + QK_ROPE
V_HEAD = 128
Q_LORA = 512
KV_LORA = 256
POOL_WINDOWS = (2, 4, 8, 16)
POOL_HIST = max(POOL_WINDOWS) - 1
ROPE_THETA = 10000.0
EPS = 1e-6
SCALE = QK_HEAD ** -0.5

LANES = 128
SUBLANES = 8
HEAD_PAD = 2 * LANES
HIST_PAD = 2 * SUBLANES
VMEM_LIMIT_BYTES = 56 * 1024 * 1024

PROMPT_TILE = 512
ATTN_TILE = 512
FFN_TOKEN_TILE = 512
FFN_HIDDEN_TILE = 512
CHUNK_PAGES = 8
CHUNK_SUB = 512

_NT = (((1,), (1,)), ((), ()))


def _rms(x, w):
    return x * lax.rsqrt(jnp.mean(x * x, axis=-1, keepdims=True) + EPS) * w


def _rows(a, n):
    r = a.shape[0]
    if r == 1 or r == n:
        return a
    return jnp.tile(a, (n // r, 1))


def _rope(blk, cosx, sinx):
    return blk * cosx + pltpu.roll(blk, LANES // 2, axis=1) * sinx


def _params(semantics):
    return pltpu.CompilerParams(dimension_semantics=semantics, vmem_limit_bytes=VMEM_LIMIT_BYTES)


def _mod_kernel(c_ref, w_ref, b_ref, o_ref):
    c = c_ref[...]
    a = (c * jax.nn.sigmoid(c)).astype(BF16)
    o_ref[...] = jnp.dot(a, w_ref[...].astype(BF16), preferred_element_type=F32) + b_ref[...]


def _modulation(c_all, w_mod, b_mod):
    n, d = c_all.shape
    width = w_mod.shape[1]
    tn = 1024
    return pl.pallas_call(
        _mod_kernel,
        out_shape=jax.ShapeDtypeStruct((n, width), F32),
        grid=(width // tn,),
        in_specs=[
            pl.BlockSpec((n, d), lambda j: (0, 0)),
            pl.BlockSpec((d, tn), lambda j: (0, j)),
            pl.BlockSpec((1, tn), lambda j: (0, j)),
        ],
        out_specs=pl.BlockSpec((n, tn), lambda j: (0, j)),
        compiler_params=_params(("arbitrary",)),
        name="modulation",
    )(c_all, w_mod, b_mod)


def _pool_prompt(u, ubuf, s_idx, tm, wpool_ref, bpool_ref, pscale_ref, pool_o_ref):
    pw = u.shape[1]
    pg = pw // len(POOL_WINDOWS)

    @pl.when(s_idx == 0)
    def _():
        ubuf[0:HIST_PAD, :] = jnp.zeros((HIST_PAD, pw), F32)

    ubuf[HIST_PAD:HIST_PAD + tm, :] = u
    pos = s_idx * tm + lax.broadcasted_iota(jnp.int32, (tm, 1), 0)
    for g, w in enumerate(POOL_WINDOWS):
        c0 = g * pg
        ug = u[:, c0:c0 + pg]
        acc = ug
        for j in range(1, w):
            acc = acc + ubuf[HIST_PAD - j:HIST_PAD - j + tm, c0:c0 + pg]
        cnt = jnp.minimum(pos + 1, w).astype(F32)
        d = acc / cnt - ug
        o = jnp.dot(d.astype(BF16), wpool_ref[g], preferred_element_type=F32) + bpool_ref[:, c0:c0 + pg]
        pool_o_ref[:, c0:c0 + pg] = (o * pscale_ref[:, c0:c0 + pg]).astype(BF16)
    ubuf[0:HIST_PAD, :] = ubuf[tm:tm + HIST_PAD, :]


def _pre_kernel(*refs, sample, tm):
    (x_ref, sh1_ref, sc1_ref, norm1_ref, win_ref, qan_ref, wq_ref, kvan_ref, cos_ref, sin_ref, qg_ref, kg_ref,
     *rest) = refs
    if sample:
        wkt_ref, qabs_ref, qr_ref, lat_ref, kr_ref, u_ref = rest
    else:
        (wkv_ref, wpool_ref, bpool_ref, pscale_ref,
         q_ref, k_ref, v_ref, lat_ref, kr_ref, pool_o_ref, utail_ref, ubuf) = rest

    x = x_ref[...]
    h = _rms(x, norm1_ref[...]) * (1.0 + _rows(sc1_ref[...], tm)) + _rows(sh1_ref[...], tm)
    z = jnp.dot(h.astype(BF16), win_ref[...], preferred_element_type=F32)
    o_kv = Q_LORA
    o_u = o_kv + KV_LORA
    o_kr = z.shape[1] - LANES
    cq = z[:, 0:o_kv]
    ckv = z[:, o_kv:o_u]
    u = z[:, o_u:o_kr]
    cosx = cos_ref[...]
    sinx = sin_ref[...]

    lat = _rms(ckv, kvan_ref[...])
    lat_ref[...] = lat
    kr = _rope(z[:, o_kr:], cosx, sinx)
    kr_ref[...] = kr[:, 0:QK_ROPE]

    q = jnp.dot(_rms(cq, qan_ref[...]).astype(BF16), wq_ref[...], preferred_element_type=F32)
    qg = qg_ref[...]
    kg = kg_ref[...]
    if not sample:
        kv = jnp.dot(lat.astype(BF16), wkv_ref[...], preferred_element_type=F32)
        kr_ss = jnp.sum(kr * kr, axis=-1, keepdims=True)

    for hd in range(N_HEADS):
        c0 = hd * HEAD_PAD
        qn = q[:, c0:c0 + QK_NOPE]
        qr = _rope(q[:, c0 + QK_NOPE:c0 + HEAD_PAD], cosx, sinx)
        ss = jnp.sum(qn * qn, axis=-1, keepdims=True) + jnp.sum(qr * qr, axis=-1, keepdims=True)
        r = lax.rsqrt(ss * (1.0 / QK_HEAD) + EPS)
        qn = qn * r * qg[:, 0:QK_NOPE]
        qr = qr * r * qg[:, QK_NOPE:]
        if sample:
            qk = (qn * kg[:, 0:QK_NOPE]).astype(BF16)
            qabs = jnp.dot(qk, wkt_ref[hd * QK_NOPE:(hd + 1) * QK_NOPE, :], preferred_element_type=F32)
            qabs_ref[hd] = qabs.astype(BF16)
            qr_ref[hd] = (qr * kg[:, QK_NOPE:]).astype(BF16)
        else:
            q_ref[hd] = jnp.concatenate([qn, qr], axis=-1).astype(BF16)
            kn = kv[:, c0:c0 + QK_NOPE]
            rk = lax.rsqrt((jnp.sum(kn * kn, axis=-1, keepdims=True) + kr_ss) * (1.0 / QK_HEAD) + EPS)
            k_ref[hd] = jnp.concatenate([kn * rk * kg[:, 0:QK_NOPE], kr * rk * kg[:, QK_NOPE:]], axis=-1).astype(BF16)
            v_ref[hd] = kv[:, c0 + QK_NOPE:c0 + HEAD_PAD].astype(BF16)

    if sample:
        u_ref[...] = u
    else:
        s_idx = pl.program_id(1)
        _pool_prompt(u, ubuf, s_idx, tm, wpool_ref, bpool_ref, pscale_ref, pool_o_ref)

        @pl.when(s_idx == pl.num_programs(1) - 1)
        def _():
            utail_ref[...] = u[tm - HIST_PAD:tm, :]


def _const_spec(a):
    nd = a.ndim
    return pl.BlockSpec(a.shape, lambda *_: (0,) * nd)


def _pre_prompt(x, mod_p, cosx, sinx, w):
    b, s, d = x.shape
    tm = PROMPT_TILE
    pw = w["pool_scale"].shape[1]
    consts_a = [w["norm1"], w["w_in"], w["q_a_norm"], w["w_q"], w["kv_a_norm"]]
    consts_b = [w["q_gain"], w["k_gain"], w["w_kv"], w["w_pool"], w["b_pool"], w["pool_scale"]]
    tok = lambda width: pl.BlockSpec((None, tm, width), lambda i, j: (i, j, 0))
    head = lambda width: pl.BlockSpec((None, N_HEADS, tm, width), lambda i, j: (i, 0, j, 0))
    mod = lambda col: pl.BlockSpec((None, 1, d), lambda i, j: (i, 0, col))
    rope = pl.BlockSpec((tm, LANES), lambda i, j: (j, 0))
    return pl.pallas_call(
        functools.partial(_pre_kernel, sample=False, tm=tm),
        out_shape=(
            jax.ShapeDtypeStruct((b, N_HEADS, s, HEAD_PAD), BF16),
            jax.ShapeDtypeStruct((b, N_HEADS, s, HEAD_PAD), BF16),
            jax.ShapeDtypeStruct((b, N_HEADS, s, V_HEAD), BF16),
            jax.ShapeDtypeStruct((b, s, KV_LORA), F32),
            jax.ShapeDtypeStruct((b, s, QK_ROPE), F32),
            jax.ShapeDtypeStruct((b, s, pw), BF16),
            jax.ShapeDtypeStruct((b, HIST_PAD, pw), F32),
        ),
        grid=(b, s // tm),
        in_specs=[tok(d), mod(0), mod(1)] + [_const_spec(a) for a in consts_a] + [rope, rope]
        + [_const_spec(a) for a in consts_b],
        out_specs=(head(HEAD_PAD), head(HEAD_PAD), head(V_HEAD), tok(KV_LORA), tok(QK_ROPE), tok(pw),
                   pl.BlockSpec((None, HIST_PAD, pw), lambda i, j: (i, 0, 0))),
        scratch_shapes=[pltpu.VMEM((tm + HIST_PAD, pw), F32)],
        compiler_params=_params(("arbitrary", "arbitrary")),
        name="pre_prompt",
    )(x, mod_p, mod_p, *consts_a, cosx, sinx, *consts_b)


def _pre_sample(x, mod_s, cosx, sinx, w):
    n, d = x.shape
    bd = mod_s.shape[0]
    pw = w["pool_scale"].shape[1]
    consts = [w["norm1"], w["w_in"], w["q_a_norm"], w["w_q"], w["kv_a_norm"]]
    mod = lambda col: pl.BlockSpec((bd, d), lambda i: (0, col))
    full = lambda *shape: pl.BlockSpec(shape, lambda i: (0,) * len(shape))
    return pl.pallas_call(
        functools.partial(_pre_kernel, sample=True, tm=n),
        out_shape=(
            jax.ShapeDtypeStruct((N_HEADS, n, KV_LORA), BF16),
            jax.ShapeDtypeStruct((N_HEADS, n, LANES), BF16),
            jax.ShapeDtypeStruct((n, KV_LORA), F32),
            jax.ShapeDtypeStruct((n, QK_ROPE), F32),
            jax.ShapeDtypeStruct((n, pw), F32),
        ),
        grid=(1,),
        in_specs=[full(n, d), mod(0), mod(1)] + [_const_spec(a) for a in consts]
        + [full(n, LANES), full(n, LANES), _const_spec(w["q_gain"]), _const_spec(w["k_gain"]), _const_spec(w["w_kt"])],
        out_specs=(full(N_HEADS, n, KV_LORA), full(N_HEADS, n, LANES), full(n, KV_LORA), full(n, QK_ROPE),
                   full(n, pw)),
        compiler_params=_params(("arbitrary",)),
        name="pre_sample",
    )(x, mod_s, mod_s, *consts, cosx, sinx, w["q_gain"], w["k_gain"], w["w_kt"])


def _attn_kernel(q_ref, k_ref, v_ref, o_ref, *, seq, tile):
    n = seq // tile
    row = lax.broadcasted_iota(jnp.int32, (tile, tile), 0)
    col = lax.broadcasted_iota(jnp.int32, (tile, tile), 1)
    for qi in range(n):
        q = q_ref[qi * tile:(qi + 1) * tile, :]
        m = jnp.full((tile, 1), -jnp.inf, F32)
        l = jnp.zeros((tile, 1), F32)
        acc = jnp.zeros((tile, V_HEAD), F32)
        for ki in range(qi + 1):
            k = k_ref[ki * tile:(ki + 1) * tile, :]
            s = lax.dot_general(q, k, _NT, preferred_element_type=F32)
            if ki == qi:
                s = jnp.where(col <= row, s, -jnp.inf)
            m_new = jnp.maximum(m, jnp.max(s, axis=-1, keepdims=True))
            alpha = jnp.exp(m - m_new)
            p = jnp.exp(s - m_new)
            l = alpha * l + jnp.sum(p, axis=-1, keepdims=True)
            acc = alpha * acc + jnp.dot(p.astype(BF16), v_ref[ki * tile:(ki + 1) * tile, :],
                                        preferred_element_type=F32)
            m = m_new
        o_ref[qi * tile:(qi + 1) * tile, :] = (acc / l).astype(BF16)


def _attention_prompt(q, k, v):
    b, h, s, _ = q.shape
    head = lambda width: pl.BlockSpec((None, None, s, width), lambda i, j: (i, j, 0, 0))
    return pl.pallas_call(
        functools.partial(_attn_kernel, seq=s, tile=min(ATTN_TILE, s)),
        out_shape=jax.ShapeDtypeStruct((b, s, h * V_HEAD), BF16),
        grid=(b, h),
        in_specs=[head(HEAD_PAD), head(HEAD_PAD), head(V_HEAD)],
        out_specs=pl.BlockSpec((None, s, V_HEAD), lambda i, j: (i, 0, j)),
        compiler_params=_params(("arbitrary", "arbitrary")),
        name="attention_prompt",
    )(q, k, v)


def _sattn_kernel(pt_ref, q_ref, qr_ref, latn_ref, krn_ref, wkt_ref, clat_hbm, ckr_hbm, ctx_ref,
                  lat_buf, kr_buf, sem, lhs, lhsr, *, layer, n_pages, page, n_new):
    b = pl.program_id(0)
    nb = pl.num_programs(0)
    n_chunks = n_pages // CHUNK_PAGES
    nq = q_ref.shape[0]
    n_k = wkt_ref.shape[0]

    def copies(seq, chunk, slot):
        out = []
        for p in range(CHUNK_PAGES):
            phys = pt_ref[seq * n_pages + chunk * CHUNK_PAGES + p]
            dst = pl.ds(p * page, page)
            out.append(pltpu.make_async_copy(clat_hbm.at[layer, phys], lat_buf.at[slot, dst, :], sem.at[0, slot]))
            out.append(pltpu.make_async_copy(ckr_hbm.at[layer, phys], kr_buf.at[slot, dst, :], sem.at[1, slot]))
        return out

    def start(seq, chunk, slot):
        for cp in copies(seq, chunk, slot):
            cp.start()

    def wait(seq, chunk, slot):
        for cp in copies(seq, chunk, slot):
            cp.wait()

    @pl.when(b == 0)
    def _():
        lhs[0:n_k, :] = wkt_ref[...]
        eye = lax.broadcasted_iota(jnp.int32, (QK_ROPE, QK_ROPE), 0) == lax.broadcasted_iota(
            jnp.int32, (QK_ROPE, QK_ROPE), 1)
        lhsr[nq:nq + QK_ROPE, :] = jnp.where(eye, 1.0, 0.0).astype(BF16)
        start(0, 0, 0)

    lhs[n_k:n_k + nq, :] = q_ref[...]
    lhsr[0:nq, :] = qr_ref[:, 0:QK_ROPE]

    def attend(lat, kr, carry, mask=None):
        m, l, acc = carry
        t = lat.shape[0]
        lat_bf = lat.astype(BF16)
        big = lax.dot_general(lhs[...], lat_bf, _NT, preferred_element_type=F32)
        kn = big[0:n_k]
        ssq = jnp.sum((kn * kn).reshape(N_HEADS, QK_NOPE, t), axis=1)
        bigr = lax.dot_general(lhsr[...], kr.astype(BF16), _NT, preferred_element_type=F32)
        krt = bigr[nq:]
        ssq = ssq + jnp.sum(krt * krt, axis=0, keepdims=True)
        r = lax.rsqrt(ssq * (1.0 / QK_HEAD) + EPS)
        s = (big[n_k:] + bigr[0:nq]) * jnp.tile(r, (nq // N_HEADS, 1))
        if mask is not None:
            s = jnp.where(mask, s, -jnp.inf)
        m_new = jnp.maximum(m, jnp.max(s, axis=-1, keepdims=True))
        alpha = jnp.exp(m - m_new)
        p = jnp.exp(s - m_new)
        l = alpha * l + jnp.sum(p, axis=-1, keepdims=True)
        acc = alpha * acc + jnp.dot(p.astype(BF16), lat_bf, preferred_element_type=F32)
        return m_new, l, acc

    pad = LANES - latn_ref.shape[0]
    lat_new = jnp.concatenate([latn_ref[...], jnp.zeros((pad, KV_LORA), F32)], axis=0)
    kr_new = jnp.concatenate([krn_ref[...], jnp.zeros((pad, QK_ROPE), F32)], axis=0)
    key = lax.broadcasted_iota(jnp.int32, (nq, LANES), 1)
    tok = lax.broadcasted_iota(jnp.int32, (nq, LANES), 0) // N_HEADS
    carry = (jnp.full((nq, 1), -jnp.inf, F32), jnp.zeros((nq, 1), F32), jnp.zeros((nq, KV_LORA), F32))
    carry = attend(lat_new, kr_new, carry, mask=(key <= tok) & (key < n_new))

    def two_chunks(i, carry):
        for slot in range(2):
            chunk = 2 * i + slot
            wait(b, chunk, slot)
            if slot == 0:
                start(b, chunk + 1, 1)
            else:
                @pl.when(chunk + 1 < n_chunks)
                def _():
                    start(b, chunk + 1, 0)

                @pl.when((chunk + 1 == n_chunks) & (b + 1 < nb))
                def _():
                    start(b + 1, 0, 0)
            for c0 in range(0, CHUNK_PAGES * page, CHUNK_SUB):
                carry = attend(lat_buf[slot, c0:c0 + CHUNK_SUB, :], kr_buf[slot, c0:c0 + CHUNK_SUB, :], carry)
        return carry

    m, l, acc = lax.fori_loop(0, n_chunks // 2, two_chunks, carry)
    ctx_ref[...] = acc / l


def _attention_sample(page_table, qabs, qr, lat_new, kr_new, w_kt, cache_lat, cache_kr, layer, n_new):
    bd, nq, _ = qabs.shape
    n_pages = page_table.shape[1]
    page = cache_lat.shape[2]
    assert n_pages % (2 * CHUNK_PAGES) == 0 and (CHUNK_PAGES * page) % CHUNK_SUB == 0
    rows = CHUNK_PAGES * page
    n_k = w_kt.shape[0]
    per_seq = lambda r, c: pl.BlockSpec((None, r, c), lambda i, pt: (i, 0, 0))
    grid_spec = pltpu.PrefetchScalarGridSpec(
        num_scalar_prefetch=1,
        grid=(bd,),
        in_specs=[
            per_seq(nq, KV_LORA), per_seq(nq, LANES), per_seq(SUBLANES, KV_LORA), per_seq(SUBLANES, QK_ROPE),
            pl.BlockSpec((n_k, KV_LORA), lambda i, pt: (0, 0)),
            pl.BlockSpec(memory_space=pl.ANY), pl.BlockSpec(memory_space=pl.ANY),
        ],
        out_specs=per_seq(nq, KV_LORA),
        scratch_shapes=[
            pltpu.VMEM((2, rows, KV_LORA), F32),
            pltpu.VMEM((2, rows, QK_ROPE), F32),
            pltpu.SemaphoreType.DMA((2, 2)),
            pltpu.VMEM((n_k + nq, KV_LORA), BF16),
            pltpu.VMEM((nq + QK_ROPE, QK_ROPE), BF16),
        ],
    )
    return pl.pallas_call(
        functools.partial(_sattn_kernel, layer=layer, n_pages=n_pages, page=page, n_new=n_new),
        out_shape=jax.ShapeDtypeStruct((bd, nq, KV_LORA), F32),
        grid_spec=grid_spec,
        compiler_params=_params(("arbitrary",)),
        name="attention_sample",
    )(page_table.reshape(-1), qabs, qr, lat_new, kr_new, w_kt, cache_lat, cache_kr)


def _ctx_to_heads_kernel(ctx_ref, wv_ref, o_ref):
    n_tok, n_head, bd, _ = ctx_ref.shape
    for t in range(n_tok):
        for hd in range(n_head):
            o = jnp.dot(ctx_ref[t, hd].astype(BF16), wv_ref[hd], preferred_element_type=F32)
            o_ref[t * bd:(t + 1) * bd, hd * V_HEAD:(hd + 1) * V_HEAD] = o.astype(BF16)


def _ctx_to_heads(ctx, w_v):
    n_tok, n_head, bd, _ = ctx.shape
    return pl.pallas_call(
        _ctx_to_heads_kernel,
        out_shape=jax.ShapeDtypeStruct((n_tok * bd, n_head * V_HEAD), BF16),
        grid=(1,),
        in_specs=[_const_spec(ctx), _const_spec(w_v)],
        out_specs=pl.BlockSpec((n_tok * bd, n_head * V_HEAD), lambda i: (0, 0)),
        compiler_params=_params(("arbitrary",)),
        name="ctx_to_heads",
    )(ctx, w_v)


def _pool_sample_kernel(u_ref, hist_ref, wpool_ref, bpool_ref, pscale_ref, o_ref, *, first_pos):
    n_tok, bd, pw = u_ref.shape
    n_hist = hist_ref.shape[0]
    pg = pw // len(POOL_WINDOWS)

    def row(i, c0):
        return hist_ref[i, :, c0:c0 + pg] if i < n_hist else u_ref[i - n_hist, :, c0:c0 + pg]

    for t in range(n_tok):
        for g, w in enumerate(POOL_WINDOWS):
            c0 = g * pg
            ug = u_ref[t, :, c0:c0 + pg]
            acc = ug
            for j in range(1, w):
                acc = acc + row(n_hist + t - j, c0)
            d = acc / float(min(first_pos + t + 1, w)) - ug
            o = jnp.dot(d.astype(BF16), wpool_ref[g], preferred_element_type=F32) + bpool_ref[:, c0:c0 + pg]
            o_ref[t * bd:(t + 1) * bd, c0:c0 + pg] = (o * pscale_ref[:, c0:c0 + pg]).astype(BF16)


def _pool_sample(u, hist, w, first_pos):
    n_tok, bd, pw = u.shape
    args = (u, hist, w["w_pool"], w["b_pool"], w["pool_scale"])
    return pl.pallas_call(
        functools.partial(_pool_sample_kernel, first_pos=first_pos),
        out_shape=jax.ShapeDtypeStruct((n_tok * bd, pw), BF16),
        grid=(1,),
        in_specs=[_const_spec(a) for a in args],
        out_specs=pl.BlockSpec((n_tok * bd, pw), lambda i: (0, 0)),
        compiler_params=_params(("arbitrary",)),
        name="pool_sample",
    )(*args)


def _mix_kernel(x_ref, a_ref, p_ref, wout_ref, g1_ref, sh2_ref, sc2_ref, norm2_ref, x1_ref, h2_ref):
    tm = x_ref.shape[0]
    aw = a_ref.shape[1]
    mix = jnp.dot(a_ref[...], wout_ref[0:aw, :], preferred_element_type=F32)
    mix = mix + jnp.dot(p_ref[...], wout_ref[aw:, :], preferred_element_type=F32)
    x1 = x_ref[...] + _rows(g1_ref[...], tm) * mix
    x1_ref[...] = x1
    h2 = _rms(x1, norm2_ref[...]) * (1.0 + _rows(sc2_ref[...], tm)) + _rows(sh2_ref[...], tm)
    h2_ref[...] = h2.astype(BF16)


def _ffn_body(h_ref, x1_ref, g2_ref, wg_ref, wu_ref, wd_ref, y_ref, acc_ref, *, axis):
    j = pl.program_id(axis)
    h = h_ref[...]
    gate = jnp.dot(h, wg_ref[...], preferred_element_type=F32)
    up = jnp.dot(h, wu_ref[...], preferred_element_type=F32)
    act = (gate * jax.nn.sigmoid(gate) * up).astype(BF16)
    part = jnp.dot(act, wd_ref[...], preferred_element_type=F32)

    @pl.when(j == 0)
    def _():
        acc_ref[...] = part

    @pl.when(j > 0)
    def _():
        acc_ref[...] += part

    @pl.when(j == pl.num_programs(axis) - 1)
    def _():
        y_ref[...] = x1_ref[...] + _rows(g2_ref[...], y_ref.shape[0]) * acc_ref[...]


def _layer_out_prompt(x, attn_o, pool_o, mod_p, w):
    b, s, d = x.shape
    tm = PROMPT_TILE
    tok = lambda width: pl.BlockSpec((None, tm, width), lambda i, j: (i, j, 0))
    mod = lambda col: pl.BlockSpec((None, 1, d), lambda i, j: (i, 0, col))
    x1, h2 = pl.pallas_call(
        _mix_kernel,
        out_shape=(jax.ShapeDtypeStruct((b, s, d), F32), jax.ShapeDtypeStruct((b, s, d), BF16)),
        grid=(b, s // tm),
        in_specs=[tok(d), tok(attn_o.shape[2]), tok(pool_o.shape[2]), _const_spec(w["w_out"]),
                  mod(2), mod(3), mod(4), _const_spec(w["norm2"])],
        out_specs=(tok(d), tok(d)),
        compiler_params=_params(("arbitrary", "arbitrary")),
        name="mix_prompt",
    )(x, attn_o, pool_o, w["w_out"], mod_p, mod_p, mod_p, w["norm2"])

    tm, tf = FFN_TOKEN_TILE, FFN_HIDDEN_TILE
    dff = w["w_gate"].shape[1]
    tok3 = pl.BlockSpec((None, tm, d), lambda i, j, f: (i, j, 0))
    return pl.pallas_call(
        functools.partial(_ffn_body, axis=2),
        out_shape=jax.ShapeDtypeStruct((b, s, d), F32),
        grid=(b, s // tm, dff // tf),
        in_specs=[tok3, tok3, pl.BlockSpec((None, 1, d), lambda i, j, f: (i, 0, 5)),
                  pl.BlockSpec((d, tf), lambda i, j, f: (0, f)),
                  pl.BlockSpec((d, tf), lambda i, j, f: (0, f)),
                  pl.BlockSpec((tf, d), lambda i, j, f: (f, 0))],
        out_specs=tok3,
        scratch_shapes=[pltpu.VMEM((tm, d), F32)],
        compiler_params=_params(("arbitrary", "arbitrary", "arbitrary")),
        name="ffn_prompt",
    )(h2, x1, mod_p, w["w_gate"], w["w_up"], w["w_down"])


def _layer_out_sample(x, attn_o, pool_o, mod_s, w):
    n, d = x.shape
    bd = mod_s.shape[0]
    full = lambda *shape: pl.BlockSpec(shape, lambda *_: (0,) * len(shape))
    mod = lambda col: pl.BlockSpec((bd, d), lambda *_: (0, col))
    x1, h2 = pl.pallas_call(
        _mix_kernel,
        out_shape=(jax.ShapeDtypeStruct((n, d), F32), jax.ShapeDtypeStruct((n, d), BF16)),
        grid=(1,),
        in_specs=[full(n, d), full(*attn_o.shape), full(*pool_o.shape), _const_spec(w["w_out"]),
                  mod(2), mod(3), mod(4), _const_spec(w["norm2"])],
        out_specs=(full(n, d), full(n, d)),
        compiler_params=_params(("arbitrary",)),
        name="mix_sample",
    )(x, attn_o, pool_o, w["w_out"], mod_s, mod_s, mod_s, w["norm2"])

    tf = FFN_HIDDEN_TILE
    dff = w["w_gate"].shape[1]
    return pl.pallas_call(
        functools.partial(_ffn_body, axis=0),
        out_shape=jax.ShapeDtypeStruct((n, d), F32),
        grid=(dff // tf,),
        in_specs=[full(n, d), full(n, d), mod(5),
                  pl.BlockSpec((d, tf), lambda f: (0, f)),
                  pl.BlockSpec((d, tf), lambda f: (0, f)),
                  pl.BlockSpec((tf, d), lambda f: (f, 0))],
        out_specs=full(n, d),
        scratch_shapes=[pltpu.VMEM((n, d), F32)],
        compiler_params=_params(("arbitrary",)),
        name="ffn_sample",
    )(h2, x1, mod_s, w["w_gate"], w["w_up"], w["w_down"])


def _rope_tables(pos):
    inv = 1.0 / (ROPE_THETA ** (jnp.arange(0, QK_ROPE, 2, dtype=F32) / QK_ROPE))
    ang = pos.astype(F32)[:, None] * inv[None, :]
    c, s = jnp.cos(ang), jnp.sin(ang)
    z = jnp.zeros((pos.shape[0], LANES - QK_ROPE), F32)
    return jnp.concatenate([c, c, z], axis=1), jnp.concatenate([-s, s, z], axis=1)


def _layer_weights(l, w_in, q_a_norm, w_q_b, kv_a_norm, w_kv_b, q_norm_nope, q_norm_rope, k_norm_nope, k_norm_rope,
                   w_pool, b_pool, pool_scale, w_out, norm1, norm2, w_gate, w_up, w_down):
    half = QK_ROPE // 2
    o_kr = Q_LORA + KV_LORA
    o_u = o_kr + QK_ROPE
    win = w_in[l]
    w_in_p = jnp.concatenate([win[:, :o_kr], win[:, o_u:], win[:, o_kr:o_u],
                              win[:, o_kr + half:o_u], win[:, o_kr:o_kr + half]], axis=1)
    wq = w_q_b[l].reshape(Q_LORA, N_HEADS, QK_HEAD)
    w_q_p = jnp.concatenate([wq, wq[:, :, QK_NOPE + half:], wq[:, :, QK_NOPE:QK_NOPE + half]], axis=2)
    wkv = w_kv_b[l].reshape(KV_LORA, N_HEADS, QK_NOPE + V_HEAD)
    zpad = jnp.zeros((HEAD_PAD - QK_HEAD,), F32)
    row = lambda a: a.reshape(1, -1)
    return {
        "norm1": row(norm1[l]), "norm2": row(norm2[l]),
        "w_in": w_in_p.astype(BF16),
        "q_a_norm": row(q_a_norm[l]), "kv_a_norm": row(kv_a_norm[l]),
        "w_q": w_q_p.reshape(Q_LORA, N_HEADS * HEAD_PAD).astype(BF16),
        "w_kv": w_kv_b[l].astype(BF16),
        "w_kt": wkv[:, :, :QK_NOPE].reshape(KV_LORA, N_HEADS * QK_NOPE).T.astype(BF16),
        "w_v": wkv[:, :, QK_NOPE:].transpose(1, 0, 2).astype(BF16),
        "q_gain": row(jnp.concatenate([q_norm_nope[l], q_norm_rope[l], q_norm_rope[l], zpad]) * SCALE),
        "k_gain": row(jnp.concatenate([k_norm_nope[l], k_norm_rope[l], k_norm_rope[l], zpad])),
        "w_pool": w_pool[l].astype(BF16), "b_pool": row(b_pool[l]), "pool_scale": row(pool_scale[l]),
        "w_out": w_out[l].astype(BF16),
        "w_gate": w_gate[l].astype(BF16), "w_up": w_up[l].astype(BF16), "w_down": w_down[l].astype(BF16),
    }


def kernel(x_prompt, x_sample, cache_kv_latent, cache_k_rope, state_pool, page_table, c_prompt, c_sample,
           w_mod, b_mod, norm1, w_in, q_a_norm, w_q_b, kv_a_norm, w_kv_b,
           q_norm_nope, q_norm_rope, k_norm_nope, k_norm_rope,
           w_pool, b_pool, pool_scale, w_out, norm2, w_gate, w_up, w_down):
    b, s, d = x_prompt.shape
    bd, t, _ = x_sample.shape
    depth = w_mod.shape[0]
    past = page_table.shape[1] * cache_kv_latent.shape[2]
    cos_p, sin_p = _rope_tables(jnp.arange(s))
    cos_s, sin_s = _rope_tables(jnp.repeat(past + jnp.arange(t), bd))
    c_all = jnp.concatenate([c_prompt, c_sample], axis=0)

    yp = x_prompt
    ys = x_sample.transpose(1, 0, 2).reshape(t * bd, d)
    outs = [[] for _ in range(6)]
    for l in range(depth):
        w = _layer_weights(l, w_in, q_a_norm, w_q_b, kv_a_norm, w_kv_b, q_norm_nope, q_norm_rope, k_norm_nope,
                           k_norm_rope, w_pool, b_pool, pool_scale, w_out, norm1, norm2, w_gate, w_up, w_down)
        mod = _modulation(c_all, w_mod[l], b_mod[l].reshape(1, -1))
        mod_p = mod[:b].reshape(b, 1, -1)
        mod_s = mod[b:]

        q, k, v, lat_p, kr_p, pool_p, utail = _pre_prompt(yp, mod_p, cos_p, sin_p, w)
        attn_p = _attention_prompt(q, k, v)
        yp = _layer_out_prompt(yp, attn_p, pool_p, mod_p, w)

        qabs, qr, lat_s, kr_s, u_s = _pre_sample(ys, mod_s, cos_s, sin_s, w)
        per_seq = lambda a: a.reshape(N_HEADS, t, bd, -1).transpose(2, 1, 0, 3).reshape(bd, t * N_HEADS, -1)
        lat_s = lat_s.reshape(t, bd, -1).transpose(1, 0, 2)
        kr_s = kr_s.reshape(t, bd, -1).transpose(1, 0, 2)
        pad_new = lambda a: jnp.pad(a, ((0, 0), (0, SUBLANES - t), (0, 0)))
        ctx = _attention_sample(page_table, per_seq(qabs), per_seq(qr), pad_new(lat_s), pad_new(kr_s), w["w_kt"],
                                cache_kv_latent, cache_k_rope, l, t)
        attn_s = _ctx_to_heads(ctx.reshape(bd, t, N_HEADS, -1).transpose(1, 2, 0, 3), w["w_v"])
        u_s = u_s.reshape(t, bd, -1)
        pool_s = _pool_sample(u_s, state_pool[l].transpose(1, 0, 2), w, past)
        ys = _layer_out_sample(ys, attn_s, pool_s, mod_s, w)

        full_s = jnp.concatenate([state_pool[l], u_s.transpose(1, 0, 2)], axis=1)
        for dst, val in zip(outs, (lat_p, kr_p, utail[:, HIST_PAD - POOL_HIST:], lat_s, kr_s,
                                   full_s[:, -POOL_HIST:])):
            dst.append(val)

    ys = ys.reshape(t, bd, d).transpose(1, 0, 2)
    return (yp, ys) + tuple(jnp.stack(o) for o in outs)
```

```python
import functools

import jax
import jax.numpy as jnp
from jax import lax
from jax.experimental import pallas as pl
from jax.experimental.pallas import tpu as pltpu

F32 = jnp.float32
BF16 = jnp.bfloat16

N_HEADS = 8
QK_NOPE = 128
QK_ROPE = 64
QK_HEAD = QK_NOPE + QK_ROPE
V_HEAD = 128
Q_LORA = 512
KV_LORA = 256
POOL_WINDOWS = (2, 4, 8, 16)
POOL_HIST = max(POOL_WINDOWS) - 1
ROPE_THETA = 10000.0
EPS = 1e-6
SCALE = QK_HEAD ** -0.5

LANES = 128
SUBLANES = 8
HEAD_PAD = 2 * LANES
HIST_PAD = 2 * SUBLANES
VMEM_LIMIT_BYTES = 56 * 1024 * 1024

PROMPT_TILE = 512
ATTN_TILE = 512
FFN_TOKEN_TILE = 512
FFN_HIDDEN_TILE = 512
CHUNK_PAGES = 8
CACHE_SLOTS = 3

_NT = (((1,), (1,)), ((), ()))


def _rms(x, w):
    return x * lax.rsqrt(jnp.mean(x * x, axis=-1, keepdims=True) + EPS) * w


def _rows(a, n):
    r = a.shape[0]
    if r == 1 or r == n:
        return a
    return jnp.tile(a, (n // r, 1))


def _rope(blk, cosx, sinx):
    return blk * cosx + pltpu.roll(blk, LANES // 2, axis=1) * sinx


def _params(semantics):
    return pltpu.CompilerParams(dimension_semantics=semantics, vmem_limit_bytes=VMEM_LIMIT_BYTES)


def _mod_kernel(c_ref, w_ref, b_ref, o_ref):
    c = c_ref[...]
    a = (c * jax.nn.sigmoid(c)).astype(BF16)
    o_ref[...] = jnp.dot(a, w_ref[...].astype(BF16), preferred_element_type=F32) + b_ref[...]


def _modulation(c_all, w_mod, b_mod):
    n, d = c_all.shape
    width = w_mod.shape[1]
    tn = 1024
    return pl.pallas_call(
        _mod_kernel,
        out_shape=jax.ShapeDtypeStruct((n, width), F32),
        grid=(width // tn,),
        in_specs=[
            pl.BlockSpec((n, d), lambda j: (0, 0)),
            pl.BlockSpec((d, tn), lambda j: (0, j)),
            pl.BlockSpec((1, tn), lambda j: (0, j)),
        ],
        out_specs=pl.BlockSpec((n, tn), lambda j: (0, j)),
        compiler_params=_params(("arbitrary",)),
        name="modulation",
    )(c_all, w_mod, b_mod)


def _pool_prompt(u, ubuf, s_idx, tm, wpool_ref, bpool_ref, pscale_ref, pool_o_ref):
    pw = u.shape[1]
    pg = pw // len(POOL_WINDOWS)

    @pl.when(s_idx == 0)
    def _():
        ubuf[0:HIST_PAD, :] = jnp.zeros((HIST_PAD, pw), F32)

    ubuf[HIST_PAD:HIST_PAD + tm, :] = u
    pos = s_idx * tm + lax.broadcasted_iota(jnp.int32, (tm, 1), 0)
    for g, w in enumerate(POOL_WINDOWS):
        c0 = g * pg
        ug = u[:, c0:c0 + pg]
        acc = ug
        for j in range(1, w):
            acc = acc + ubuf[HIST_PAD - j:HIST_PAD - j + tm, c0:c0 + pg]
        cnt = jnp.minimum(pos + 1, w).astype(F32)
        d = acc / cnt - ug
        o = jnp.dot(d.astype(BF16), wpool_ref[g], preferred_element_type=F32) + bpool_ref[:, c0:c0 + pg]
        pool_o_ref[:, c0:c0 + pg] = (o * pscale_ref[:, c0:c0 + pg]).astype(BF16)
    ubuf[0:HIST_PAD, :] = ubuf[tm:tm + HIST_PAD, :]


def _pre_kernel(*refs, sample, tm):
    (x_ref, sh1_ref, sc1_ref, norm1_ref, win_ref, qan_ref, wq_ref, kvan_ref, cos_ref, sin_ref, qg_ref, kg_ref,
     *rest) = refs
    if sample:
        wkt_ref, qabs_ref, qr_ref, lat_ref, kr_ref, u_ref = rest
    else:
        (wkv_ref, wpool_ref, bpool_ref, pscale_ref,
         q_ref, k_ref, v_ref, lat_ref, kr_ref, pool_o_ref, utail_ref, ubuf) = rest

    x = x_ref[...]
    h = _rms(x, norm1_ref[...]) * (1.0 + _rows(sc1_ref[...], tm)) + _rows(sh1_ref[...], tm)
    z = jnp.dot(h.astype(BF16), win_ref[...], preferred_element_type=F32)
    o_kv = Q_LORA
    o_u = o_kv + KV_LORA
    o_kr = z.shape[1] - LANES
    cq = z[:, 0:o_kv]
    ckv = z[:, o_kv:o_u]
    u = z[:, o_u:o_kr]
    cosx = cos_ref[...]
    sinx = sin_ref[...]

    lat = _rms(ckv, kvan_ref[...])
    lat_ref[...] = lat
    kr = _rope(z[:, o_kr:], cosx, sinx)
    kr_ref[...] = kr[:, 0:QK_ROPE]

    q = jnp.dot(_rms(cq, qan_ref[...]).astype(BF16), wq_ref[...], preferred_element_type=F32)
    qg = qg_ref[...]
    kg = kg_ref[...]
    if not sample:
        kv = jnp.dot(lat.astype(BF16), wkv_ref[...], preferred_element_type=F32)
        kr_ss = jnp.sum(kr * kr, axis=-1, keepdims=True)

    for hd in range(N_HEADS):
        c0 = hd * HEAD_PAD
        qn = q[:, c0:c0 + QK_NOPE]
        qr = _rope(q[:, c0 + QK_NOPE:c0 + HEAD_PAD], cosx, sinx)
        ss = jnp.sum(qn * qn, axis=-1, keepdims=True) + jnp.sum(qr * qr, axis=-1, keepdims=True)
        r = lax.rsqrt(ss * (1.0 / QK_HEAD) + EPS)
        qn = qn * r * qg[:, 0:QK_NOPE]
        qr = qr * r * qg[:, QK_NOPE:]
        if sample:
            qk = (qn * kg[:, 0:QK_NOPE]).astype(BF16)
            qabs = jnp.dot(qk, wkt_ref[hd * QK_NOPE:(hd + 1) * QK_NOPE, :], preferred_element_type=F32)
            qabs_ref[hd] = qabs.astype(BF16)
            qr_ref[hd] = (qr * kg[:, QK_NOPE:]).astype(BF16)
        else:
            q_ref[hd] = jnp.concatenate([qn, qr], axis=-1).astype(BF16)
            kn = kv[:, c0:c0 + QK_NOPE]
            rk = lax.rsqrt((jnp.sum(kn * kn, axis=-1, keepdims=True) + kr_ss) * (1.0 / QK_HEAD) + EPS)
            k_ref[hd] = jnp.concatenate([kn * rk * kg[:, 0:QK_NOPE], kr * rk * kg[:, QK_NOPE:]], axis=-1).astype(BF16)
            v_ref[hd] = kv[:, c0 + QK_NOPE:c0 + HEAD_PAD].astype(BF16)

    if sample:
        u_ref[...] = u
    else:
        s_idx = pl.program_id(1)
        _pool_prompt(u, ubuf, s_idx, tm, wpool_ref, bpool_ref, pscale_ref, pool_o_ref)

        @pl.when(s_idx == pl.num_programs(1) - 1)
        def _():
            utail_ref[...] = u[tm - HIST_PAD:tm, :]


def _const_spec(a):
    nd = a.ndim
    return pl.BlockSpec(a.shape, lambda *_: (0,) * nd)


def _pre_prompt(x, mod_p, cosx, sinx, w):
    b, s, d = x.shape
    tm = PROMPT_TILE
    pw = w["pool_scale"].shape[1]
    consts_a = [w["norm1"], w["w_in"], w["q_a_norm"], w["w_q"], w["kv_a_norm"]]
    consts_b = [w["q_gain"], w["k_gain"], w["w_kv"], w["w_pool"], w["b_pool"], w["pool_scale"]]
    tok = lambda width: pl.BlockSpec((None, tm, width), lambda i, j: (i, j, 0))
    head = lambda width: pl.BlockSpec((None, N_HEADS, tm, width), lambda i, j: (i, 0, j, 0))
    mod = lambda col: pl.BlockSpec((None, 1, d), lambda i, j: (i, 0, col))
    rope = pl.BlockSpec((tm, LANES), lambda i, j: (j, 0))
    return pl.pallas_call(
        functools.partial(_pre_kernel, sample=False, tm=tm),
        out_shape=(
            jax.ShapeDtypeStruct((b, N_HEADS, s, HEAD_PAD), BF16),
            jax.ShapeDtypeStruct((b, N_HEADS, s, HEAD_PAD), BF16),
            jax.ShapeDtypeStruct((b, N_HEADS, s, V_HEAD), BF16),
            jax.ShapeDtypeStruct((b, s, KV_LORA), F32),
            jax.ShapeDtypeStruct((b, s, QK_ROPE), F32),
            jax.ShapeDtypeStruct((b, s, pw), BF16),
            jax.ShapeDtypeStruct((b, HIST_PAD, pw), F32),
        ),
        grid=(b, s // tm),
        in_specs=[tok(d), mod(0), mod(1)] + [_const_spec(a) for a in consts_a] + [rope, rope]
        + [_const_spec(a) for a in consts_b],
        out_specs=(head(HEAD_PAD), head(HEAD_PAD), head(V_HEAD), tok(KV_LORA), tok(QK_ROPE), tok(pw),
                   pl.BlockSpec((None, HIST_PAD, pw), lambda i, j: (i, 0, 0))),
        scratch_shapes=[pltpu.VMEM((tm + HIST_PAD, pw), F32)],
        compiler_params=_params(("arbitrary", "arbitrary")),
        name="pre_prompt",
    )(x, mod_p, mod_p, *consts_a, cosx, sinx, *consts_b)


def _pre_sample(x, mod_s, cosx, sinx, w):
    n, d = x.shape
    bd = mod_s.shape[0]
    pw = w["pool_scale"].shape[1]
    consts = [w["norm1"], w["w_in"], w["q_a_norm"], w["w_q"], w["kv_a_norm"]]
    mod = lambda col: pl.BlockSpec((bd, d), lambda i: (0, col))
    full = lambda *shape: pl.BlockSpec(shape, lambda i: (0,) * len(shape))
    return pl.pallas_call(
        functools.partial(_pre_kernel, sample=True, tm=n),
        out_shape=(
            jax.ShapeDtypeStruct((N_HEADS, n, KV_LORA), BF16),
            jax.ShapeDtypeStruct((N_HEADS, n, LANES), BF16),
            jax.ShapeDtypeStruct((n, KV_LORA), F32),
            jax.ShapeDtypeStruct((n, QK_ROPE), F32),
            jax.ShapeDtypeStruct((n, pw), F32),
        ),
        grid=(1,),
        in_specs=[full(n, d), mod(0), mod(1)] + [_const_spec(a) for a in consts]
        + [full(n, LANES), full(n, LANES), _const_spec(w["q_gain"]), _const_spec(w["k_gain"]), _const_spec(w["w_kt"])],
        out_specs=(full(N_HEADS, n, KV_LORA), full(N_HEADS, n, LANES), full(n, KV_LORA), full(n, QK_ROPE),
                   full(n, pw)),
        compiler_params=_params(("arbitrary",)),
        name="pre_sample",
    )(x, mod_s, mod_s, *consts, cosx, sinx, w["q_gain"], w["k_gain"], w["w_kt"])


def _attn_kernel(q_ref, k_ref, v_ref, o_ref, *, seq, tile):
    n = seq // tile
    row = lax.broadcasted_iota(jnp.int32, (tile, tile), 0)
    col = lax.broadcasted_iota(jnp.int32, (tile, tile), 1)
    for qi in range(n):
        q = q_ref[qi * tile:(qi + 1) * tile, :]
        m = jnp.full((tile, 1), -jnp.inf, F32)
        l = jnp.zeros((tile, 1), F32)
        acc = jnp.zeros((tile, V_HEAD), F32)
        for ki in range(qi + 1):
            k = k_ref[ki * tile:(ki + 1) * tile, :]
            s = lax.dot_general(q, k, _NT, preferred_element_type=F32)
            if ki == qi:
                s = jnp.where(col <= row, s, -jnp.inf)
            m_new = jnp.maximum(m, jnp.max(s, axis=-1, keepdims=True))
            alpha = jnp.exp(m - m_new)
            p = jnp.exp(s - m_new)
            l = alpha * l + jnp.sum(p, axis=-1, keepdims=True)
            acc = alpha * acc + jnp.dot(p.astype(BF16), v_ref[ki * tile:(ki + 1) * tile, :],
                                        preferred_element_type=F32)
            m = m_new
        o_ref[qi * tile:(qi + 1) * tile, :] = (acc / l).astype(BF16)


def _attention_prompt(q, k, v):
    b, h, s, _ = q.shape
    head = lambda width: pl.BlockSpec((None, None, s, width), lambda i, j: (i, j, 0, 0))
    return pl.pallas_call(
        functools.partial(_attn_kernel, seq=s, tile=min(ATTN_TILE, s)),
        out_shape=jax.ShapeDtypeStruct((b, s, h * V_HEAD), BF16),
        grid=(b, h),
        in_specs=[head(HEAD_PAD), head(HEAD_PAD), head(V_HEAD)],
        out_specs=pl.BlockSpec((None, s, V_HEAD), lambda i, j: (i, 0, j)),
        compiler_params=_params(("arbitrary", "arbitrary")),
        name="attention_prompt",
    )(q, k, v)


def _sattn_kernel(pt_ref, q_ref, qr_ref, latn_ref, krn_ref, wkt_ref, clat_hbm, ckr_hbm, ctx_ref,
                  lat_buf, kr_buf, sem, lhs, s_buf, *, layer, n_pages, page, n_new):
    b = pl.program_id(0)
    n_chunks = n_pages // CHUNK_PAGES
    total = pl.num_programs(0) * n_chunks
    g0 = b * n_chunks
    nq = q_ref.shape[0]
    n_k = wkt_ref.shape[0]

    def slot_of(g):
        return lax.rem(g, CACHE_SLOTS)

    def copies(g):
        slot = slot_of(g)
        out = []
        for p in range(CHUNK_PAGES):
            phys = pt_ref[g * CHUNK_PAGES + p]
            dst = pl.ds(p * page, page)
            out.append(pltpu.make_async_copy(clat_hbm.at[layer, phys], lat_buf.at[slot, dst, :], sem.at[0, slot]))
            out.append(pltpu.make_async_copy(ckr_hbm.at[layer, phys], kr_buf.at[slot, :, dst], sem.at[1, slot]))
        return out

    def start(g):
        for cp in copies(g):
            cp.start()

    def wait(g):
        for cp in copies(g):
            cp.wait()

    @pl.when(b == 0)
    def _():
        lhs[0:n_k, :] = wkt_ref[...]
        start(0)

    lhs[n_k:n_k + nq, :] = q_ref[...]

    def scores(lat, krt):
        t = lat.shape[0]
        big = lax.dot_general(lhs[...], lat.astype(BF16), _NT, preferred_element_type=F32)
        kn = big[0:n_k]
        ssq = jnp.sum((kn * kn).reshape(N_HEADS, QK_NOPE, t), axis=1)
        ssq = ssq + jnp.sum(krt * krt, axis=0, keepdims=True)
        r = lax.rsqrt(ssq * (1.0 / QK_HEAD) + EPS)
        sr = jnp.dot(qr_ref[:, 0:QK_ROPE], krt.astype(BF16), preferred_element_type=F32)
        return (big[n_k:] + sr) * jnp.tile(r, (nq // N_HEADS, 1))

    def update(s, lat, carry):
        m, l, acc = carry
        m_new = jnp.maximum(m, jnp.max(s, axis=-1, keepdims=True))
        alpha = jnp.exp(m - m_new)
        p = jnp.exp(s - m_new)
        l = alpha * l + jnp.sum(p, axis=-1, keepdims=True)
        acc = alpha * acc + jnp.dot(p.astype(BF16), lat.astype(BF16), preferred_element_type=F32)
        return m_new, l, acc

    pad = LANES - latn_ref.shape[0]
    lat_new = jnp.concatenate([latn_ref[...], jnp.zeros((pad, KV_LORA), F32)], axis=0)
    key = lax.broadcasted_iota(jnp.int32, (nq, LANES), 1)
    tok = lax.broadcasted_iota(jnp.int32, (nq, LANES), 0) // N_HEADS
    s_new = jnp.where((key <= tok) & (key < n_new), scores(lat_new, krn_ref[...]), -jnp.inf)
    carry = (jnp.full((nq, 1), -jnp.inf, F32), jnp.zeros((nq, 1), F32), jnp.zeros((nq, KV_LORA), F32))
    carry = update(s_new, lat_new, carry)

    wait(g0)
    start(g0 + 1)
    s_buf[0] = scores(lat_buf[slot_of(g0)], kr_buf[slot_of(g0)])

    def body(c, carry):
        g = g0 + c
        wait(g)

        @pl.when(g + 1 < total)
        def _():
            start(g + 1)

        par = lax.rem(c, 2)
        s_prev = s_buf[1 - par]
        s_buf[par] = scores(lat_buf[slot_of(g)], kr_buf[slot_of(g)])
        return update(s_prev, lat_buf[slot_of(g - 1)], carry)

    carry = lax.fori_loop(1, n_chunks, body, carry)
    m, l, acc = update(s_buf[(n_chunks - 1) % 2], lat_buf[slot_of(g0 + n_chunks - 1)], carry)
    ctx_ref[...] = acc / l


def _attention_sample(page_table, qabs, qr, lat_new, krt_new, w_kt, cache_lat, cache_krt, layer, n_new):
    bd, nq, _ = qabs.shape
    n_pages = page_table.shape[1]
    page = cache_lat.shape[2]
    assert n_pages % CHUNK_PAGES == 0 and n_pages // CHUNK_PAGES >= 2
    rows = CHUNK_PAGES * page
    n_k = w_kt.shape[0]
    per_seq = lambda r, c: pl.BlockSpec((None, r, c), lambda i, pt: (i, 0, 0))
    grid_spec = pltpu.PrefetchScalarGridSpec(
        num_scalar_prefetch=1,
        grid=(bd,),
        in_specs=[
            per_seq(nq, KV_LORA), per_seq(nq, LANES), per_seq(SUBLANES, KV_LORA), per_seq(QK_ROPE, LANES),
            pl.BlockSpec((n_k, KV_LORA), lambda i, pt: (0, 0)),
            pl.BlockSpec(memory_space=pl.ANY), pl.BlockSpec(memory_space=pl.ANY),
        ],
        out_specs=per_seq(nq, KV_LORA),
        scratch_shapes=[
            pltpu.VMEM((CACHE_SLOTS, rows, KV_LORA), F32),
            pltpu.VMEM((CACHE_SLOTS, QK_ROPE, rows), F32),
            pltpu.SemaphoreType.DMA((2, CACHE_SLOTS)),
            pltpu.VMEM((n_k + nq, KV_LORA), BF16),
            pltpu.VMEM((2, nq, rows), F32),
        ],
    )
    return pl.pallas_call(
        functools.partial(_sattn_kernel, layer=layer, n_pages=n_pages, page=page, n_new=n_new),
        out_shape=jax.ShapeDtypeStruct((bd, nq, KV_LORA), F32),
        grid_spec=grid_spec,
        compiler_params=_params(("arbitrary",)),
        name="attention_sample",
    )(page_table.reshape(-1), qabs, qr, lat_new, krt_new, w_kt, cache_lat, cache_krt)


def _ctx_to_heads_kernel(ctx_ref, wv_ref, o_ref):
    n_tok, n_head, bd, _ = ctx_ref.shape
    for t in range(n_tok):
        for hd in range(n_head):
            o = jnp.dot(ctx_ref[t, hd].astype(BF16), wv_ref[hd], preferred_element_type=F32)
            o_ref[t * bd:(t + 1) * bd, hd * V_HEAD:(hd + 1) * V_HEAD] = o.astype(BF16)


def _ctx_to_heads(ctx, w_v):
    n_tok, n_head, bd, _ = ctx.shape
    return pl.pallas_call(
        _ctx_to_heads_kernel,
        out_shape=jax.ShapeDtypeStruct((n_tok * bd, n_head * V_HEAD), BF16),
        grid=(1,),
        in_specs=[_const_spec(ctx), _const_spec(w_v)],
        out_specs=pl.BlockSpec((n_tok * bd, n_head * V_HEAD), lambda i: (0, 0)),
        compiler_params=_params(("arbitrary",)),
        name="ctx_to_heads",
    )(ctx, w_v)


def _pool_sample_kernel(u_ref, hist_ref, wpool_ref, bpool_ref, pscale_ref, o_ref, *, first_pos):
    n_tok, bd, pw = u_ref.shape
    n_hist = hist_ref.shape[0]
    pg = pw // len(POOL_WINDOWS)

    def row(i, c0):
        return hist_ref[i, :, c0:c0 + pg] if i < n_hist else u_ref[i - n_hist, :, c0:c0 + pg]

    for t in range(n_tok):
        for g, w in enumerate(POOL_WINDOWS):
            c0 = g * pg
            ug = u_ref[t, :, c0:c0 + pg]
            acc = ug
            for j in range(1, w):
                acc = acc + row(n_hist + t - j, c0)
            d = acc / float(min(first_pos + t + 1, w)) - ug
            o = jnp.dot(d.astype(BF16), wpool_ref[g], preferred_element_type=F32) + bpool_ref[:, c0:c0 + pg]
            o_ref[t * bd:(t + 1) * bd, c0:c0 + pg] = (o * pscale_ref[:, c0:c0 + pg]).astype(BF16)


def _pool_sample(u, hist, w, first_pos):
    n_tok, bd, pw = u.shape
    args = (u, hist, w["w_pool"], w["b_pool"], w["pool_scale"])
    return pl.pallas_call(
        functools.partial(_pool_sample_kernel, first_pos=first_pos),
        out_shape=jax.ShapeDtypeStruct((n_tok * bd, pw), BF16),
        grid=(1,),
        in_specs=[_const_spec(a) for a in args],
        out_specs=pl.BlockSpec((n_tok * bd, pw), lambda i: (0, 0)),
        compiler_params=_params(("arbitrary",)),
        name="pool_sample",
    )(*args)


def _mix_kernel(x_ref, a_ref, p_ref, wout_ref, g1_ref, sh2_ref, sc2_ref, norm2_ref, x1_ref, h2_ref):
    tm = x_ref.shape[0]
    aw = a_ref.shape[1]
    mix = jnp.dot(a_ref[...], wout_ref[0:aw, :], preferred_element_type=F32)
    mix = mix + jnp.dot(p_ref[...], wout_ref[aw:, :], preferred_element_type=F32)
    x1 = x_ref[...] + _rows(g1_ref[...], tm) * mix
    x1_ref[...] = x1
    h2 = _rms(x1, norm2_ref[...]) * (1.0 + _rows(sc2_ref[...], tm)) + _rows(sh2_ref[...], tm)
    h2_ref[...] = h2.astype(BF16)


def _ffn_body(h_ref, x1_ref, g2_ref, wg_ref, wu_ref, wd_ref, y_ref, *, axis):
    @pl.when(pl.program_id(axis) == 0)
    def _():
        y_ref[...] = x1_ref[...]

    h = h_ref[...]
    gate = jnp.dot(h, wg_ref[...], preferred_element_type=F32)
    up = jnp.dot(h, wu_ref[...], preferred_element_type=F32)
    act = (gate * jax.nn.sigmoid(gate) * up).astype(BF16)
    part = jnp.dot(act, wd_ref[...], preferred_element_type=F32)
    y_ref[...] += _rows(g2_ref[...], y_ref.shape[0]) * part


def _layer_out_prompt(x, attn_o, pool_o, mod_p, w):
    b, s, d = x.shape
    tm = PROMPT_TILE
    tok = lambda width: pl.BlockSpec((None, tm, width), lambda i, j: (i, j, 0))
    mod = lambda col: pl.BlockSpec((None, 1, d), lambda i, j: (i, 0, col))
    x1, h2 = pl.pallas_call(
        _mix_kernel,
        out_shape=(jax.ShapeDtypeStruct((b, s, d), F32), jax.ShapeDtypeStruct((b, s, d), BF16)),
        grid=(b, s // tm),
        in_specs=[tok(d), tok(attn_o.shape[2]), tok(pool_o.shape[2]), _const_spec(w["w_out"]),
                  mod(2), mod(3), mod(4), _const_spec(w["norm2"])],
        out_specs=(tok(d), tok(d)),
        compiler_params=_params(("arbitrary", "arbitrary")),
        name="mix_prompt",
    )(x, attn_o, pool_o, w["w_out"], mod_p, mod_p, mod_p, w["norm2"])

    tm, tf = FFN_TOKEN_TILE, FFN_HIDDEN_TILE
    dff = w["w_gate"].shape[1]
    tok3 = pl.BlockSpec((None, tm, d), lambda i, j, f: (i, j, 0))
    return pl.pallas_call(
        functools.partial(_ffn_body, axis=2),
        out_shape=jax.ShapeDtypeStruct((b, s, d), F32),
        grid=(b, s // tm, dff // tf),
        in_specs=[tok3, tok3, pl.BlockSpec((None, 1, d), lambda i, j, f: (i, 0, 5)),
                  pl.BlockSpec((d, tf), lambda i, j, f: (0, f)),
                  pl.BlockSpec((d, tf), lambda i, j, f: (0, f)),
                  pl.BlockSpec((tf, d), lambda i, j, f: (f, 0))],
        out_specs=tok3,
        compiler_params=_params(("arbitrary", "arbitrary", "arbitrary")),
        name="ffn_prompt",
    )(h2, x1, mod_p, w["w_gate"], w["w_up"], w["w_down"])


def _layer_out_sample(x, attn_o, pool_o, mod_s, w):
    n, d = x.shape
    bd = mod_s.shape[0]
    full = lambda *shape: pl.BlockSpec(shape, lambda *_: (0,) * len(shape))
    mod = lambda col: pl.BlockSpec((bd, d), lambda *_: (0, col))
    x1, h2 = pl.pallas_call(
        _mix_kernel,
        out_shape=(jax.ShapeDtypeStruct((n, d), F32), jax.ShapeDtypeStruct((n, d), BF16)),
        grid=(1,),
        in_specs=[full(n, d), full(*attn_o.shape), full(*pool_o.shape), _const_spec(w["w_out"]),
                  mod(2), mod(3), mod(4), _const_spec(w["norm2"])],
        out_specs=(full(n, d), full(n, d)),
        compiler_params=_params(("arbitrary",)),
        name="mix_sample",
    )(x, attn_o, pool_o, w["w_out"], mod_s, mod_s, mod_s, w["norm2"])

    tf = FFN_HIDDEN_TILE
    dff = w["w_gate"].shape[1]
    return pl.pallas_call(
        functools.partial(_ffn_body, axis=0),
        out_shape=jax.ShapeDtypeStruct((n, d), F32),
        grid=(dff // tf,),
        in_specs=[full(n, d), full(n, d), mod(5),
                  pl.BlockSpec((d, tf), lambda f: (0, f)),
                  pl.BlockSpec((d, tf), lambda f: (0, f)),
                  pl.BlockSpec((tf, d), lambda f: (f, 0))],
        out_specs=full(n, d),
        compiler_params=_params(("arbitrary",)),
        name="ffn_sample",
    )(h2, x1, mod_s, w["w_gate"], w["w_up"], w["w_down"])


def _rope_tables(pos):
    inv = 1.0 / (ROPE_THETA ** (jnp.arange(0, QK_ROPE, 2, dtype=F32) / QK_ROPE))
    ang = pos.astype(F32)[:, None] * inv[None, :]
    c, s = jnp.cos(ang), jnp.sin(ang)
    z = jnp.zeros((pos.shape[0], LANES - QK_ROPE), F32)
    return jnp.concatenate([c, c, z], axis=1), jnp.concatenate([-s, s, z], axis=1)


def _layer_weights(l, w_in, q_a_norm, w_q_b, kv_a_norm, w_kv_b, q_norm_nope, q_norm_rope, k_norm_nope, k_norm_rope,
                   w_pool, b_pool, pool_scale, w_out, norm1, norm2, w_gate, w_up, w_down):
    half = QK_ROPE // 2
    o_kr = Q_LORA + KV_LORA
    o_u = o_kr + QK_ROPE
    win = w_in[l]
    w_in_p = jnp.concatenate([win[:, :o_kr], win[:, o_u:], win[:, o_kr:o_u],
                              win[:, o_kr + half:o_u], win[:, o_kr:o_kr + half]], axis=1)
    wq = w_q_b[l].reshape(Q_LORA, N_HEADS, QK_HEAD)
    w_q_p = jnp.concatenate([wq, wq[:, :, QK_NOPE + half:], wq[:, :, QK_NOPE:QK_NOPE + half]], axis=2)
    wkv = w_kv_b[l].reshape(KV_LORA, N_HEADS, QK_NOPE + V_HEAD)
    zpad = jnp.zeros((HEAD_PAD - QK_HEAD,), F32)
    row = lambda a: a.reshape(1, -1)
    return {
        "norm1": row(norm1[l]), "norm2": row(norm2[l]),
        "w_in": w_in_p.astype(BF16),
        "q_a_norm": row(q_a_norm[l]), "kv_a_norm": row(kv_a_norm[l]),
        "w_q": w_q_p.reshape(Q_LORA, N_HEADS * HEAD_PAD).astype(BF16),
        "w_kv": w_kv_b[l].astype(BF16),
        "w_kt": wkv[:, :, :QK_NOPE].reshape(KV_LORA, N_HEADS * QK_NOPE).T.astype(BF16),
        "w_v": wkv[:, :, QK_NOPE:].transpose(1, 0, 2).astype(BF16),
        "q_gain": row(jnp.concatenate([q_norm_nope[l], q_norm_rope[l], q_norm_rope[l], zpad]) * SCALE),
        "k_gain": row(jnp.concatenate([k_norm_nope[l], k_norm_rope[l], k_norm_rope[l], zpad])),
        "w_pool": w_pool[l].astype(BF16), "b_pool": row(b_pool[l]), "pool_scale": row(pool_scale[l]),
        "w_out": w_out[l].astype(BF16),
        "w_gate": w_gate[l].astype(BF16), "w_up": w_up[l].astype(BF16), "w_down": w_down[l].astype(BF16),
    }


def kernel(x_prompt, x_sample, cache_kv_latent, cache_k_rope, state_pool, page_table, c_prompt, c_sample,
           w_mod, b_mod, norm1, w_in, q_a_norm, w_q_b, kv_a_norm, w_kv_b,
           q_norm_nope, q_norm_rope, k_norm_nope, k_norm_rope,
           w_pool, b_pool, pool_scale, w_out, norm2, w_gate, w_up, w_down):
    b, s, d = x_prompt.shape
    bd, t, _ = x_sample.shape
    depth = w_mod.shape[0]
    past = page_table.shape[1] * cache_kv_latent.shape[2]
    cos_p, sin_p = _rope_tables(jnp.arange(s))
    cos_s, sin_s = _rope_tables(jnp.repeat(past + jnp.arange(t), bd))
    c_all = jnp.concatenate([c_prompt, c_sample], axis=0)
    cache_krt = jnp.swapaxes(cache_k_rope, 2, 3)

    yp = x_prompt
    ys = x_sample.transpose(1, 0, 2).reshape(t * bd, d)
    outs = [[] for _ in range(6)]
    for l in range(depth):
        w = _layer_weights(l, w_in, q_a_norm, w_q_b, kv_a_norm, w_kv_b, q_norm_nope, q_norm_rope, k_norm_nope,
                           k_norm_rope, w_pool, b_pool, pool_scale, w_out, norm1, norm2, w_gate, w_up, w_down)
        mod = _modulation(c_all, w_mod[l], b_mod[l].reshape(1, -1))
        mod_p = mod[:b].reshape(b, 1, -1)
        mod_s = mod[b:]

        q, k, v, lat_p, kr_p, pool_p, utail = _pre_prompt(yp, mod_p, cos_p, sin_p, w)
        attn_p = _attention_prompt(q, k, v)
        yp = _layer_out_prompt(yp, attn_p, pool_p, mod_p, w)

        qabs, qr, lat_s, kr_s, u_s = _pre_sample(ys, mod_s, cos_s, sin_s, w)
        per_seq = lambda a: a.reshape(N_HEADS, t, bd, -1).transpose(2, 1, 0, 3).reshape(bd, t * N_HEADS, -1)
        lat_s = lat_s.reshape(t, bd, -1).transpose(1, 0, 2)
        kr_s = kr_s.reshape(t, bd, -1).transpose(1, 0, 2)
        lat_new = jnp.pad(lat_s, ((0, 0), (0, SUBLANES - t), (0, 0)))
        krt_new = jnp.pad(kr_s.transpose(0, 2, 1), ((0, 0), (0, 0), (0, LANES - t)))
        ctx = _attention_sample(page_table, per_seq(qabs), per_seq(qr), lat_new, krt_new, w["w_kt"],
                                cache_kv_latent, cache_krt, l, t)
        attn_s = _ctx_to_heads(ctx.reshape(bd, t, N_HEADS, -1).transpose(1, 2, 0, 3), w["w_v"])
        u_s = u_s.reshape(t, bd, -1)
        pool_s = _pool_sample(u_s, state_pool[l].transpose(1, 0, 2), w, past)
        ys = _layer_out_sample(ys, attn_s, pool_s, mod_s, w)

        full_s = jnp.concatenate([state_pool[l], u_s.transpose(1, 0, 2)], axis=1)
        for dst, val in zip(outs, (lat_p, kr_p, utail[:, HIST_PAD - POOL_HIST:], lat_s, kr_s,
                                   full_s[:, -POOL_HIST:])):
            dst.append(val)

    ys = ys.reshape(t, bd, d).transpose(1, 0, 2)
    return (yp, ys) + tuple(jnp.stack(o) for o in outs)
```

```python
import functools

import jax
import jax.numpy as jnp
from jax import lax
from jax.experimental import pallas as pl
from jax.experimental.pallas import tpu as pltpu

F32 = jnp.float32
BF16 = jnp.bfloat16

N_HEADS = 8
QK_NOPE = 128
QK_ROPE = 64
QK_HEAD = QK_NOPE + QK_ROPE
V_HEAD = 128
Q_LORA = 512
KV_LORA = 256
POOL_WINDOWS = (2, 4, 8, 16)
POOL_HIST = max(POOL_WINDOWS) - 1
ROPE_THETA = 10000.0
EPS = 1e-6
SCALE = QK_HEAD ** -0.5

LANES = 128
SUBLANES = 8
HEAD_PAD = 2 * LANES
HIST_PAD = 2 * SUBLANES
VMEM_LIMIT_BYTES = 56 * 1024 * 1024

PROMPT_TILE = 512
ATTN_TILE = 512
FFN_TOKEN_TILE = 1024
FFN_HIDDEN_TILE = 512
CHUNK_PAGES = 16
PREFETCH_CHUNKS = 2
CACHE_SLOTS = PREFETCH_CHUNKS + 2

_NT = (((1,), (1,)), ((), ()))


def _rms(x, w):
    return x * lax.rsqrt(jnp.mean(x * x, axis=-1, keepdims=True) + EPS) * w


def _rows(a, n):
    r = a.shape[0]
    if r == 1 or r == n:
        return a
    return jnp.tile(a, (n // r, 1))


def _rope(blk, cosx, sinx):
    return blk * cosx + pltpu.roll(blk, LANES // 2, axis=1) * sinx


def _params(semantics):
    return pltpu.CompilerParams(dimension_semantics=semantics, vmem_limit_bytes=VMEM_LIMIT_BYTES)


def _mod_kernel(c_ref, w_ref, b_ref, o_ref):
    c = c_ref[...]
    a = (c * jax.nn.sigmoid(c)).astype(BF16)
    o_ref[...] = jnp.dot(a, w_ref[...].astype(BF16), preferred_element_type=F32) + b_ref[...]


def _modulation(c_all, w_mod, b_mod):
    n, d = c_all.shape
    width = w_mod.shape[1]
    tn = 1024
    return pl.pallas_call(
        _mod_kernel,
        out_shape=jax.ShapeDtypeStruct((n, width), F32),
        grid=(width // tn,),
        in_specs=[
            pl.BlockSpec((n, d), lambda j: (0, 0)),
            pl.BlockSpec((d, tn), lambda j: (0, j)),
            pl.BlockSpec((1, tn), lambda j: (0, j)),
        ],
        out_specs=pl.BlockSpec((n, tn), lambda j: (0, j)),
        compiler_params=_params(("arbitrary",)),
        name="modulation",
    )(c_all, w_mod, b_mod)


def _pool_prompt(u, ubuf, s_idx, tm, wpool_ref, bpool_ref, pscale_ref, pool_o_ref):
    pw = u.shape[1]
    pg = pw // len(POOL_WINDOWS)

    @pl.when(s_idx == 0)
    def _():
        ubuf[0:HIST_PAD, :] = jnp.zeros((HIST_PAD, pw), F32)

    ubuf[HIST_PAD:HIST_PAD + tm, :] = u
    pos = s_idx * tm + lax.broadcasted_iota(jnp.int32, (tm, 1), 0)
    for g, w in enumerate(POOL_WINDOWS):
        c0 = g * pg
        ug = u[:, c0:c0 + pg]
        acc = ug
        for j in range(1, w):
            acc = acc + ubuf[HIST_PAD - j:HIST_PAD - j + tm, c0:c0 + pg]
        cnt = jnp.minimum(pos + 1, w).astype(F32)
        d = acc / cnt - ug
        o = jnp.dot(d.astype(BF16), wpool_ref[g], preferred_element_type=F32) + bpool_ref[:, c0:c0 + pg]
        pool_o_ref[:, c0:c0 + pg] = (o * pscale_ref[:, c0:c0 + pg]).astype(BF16)
    ubuf[0:HIST_PAD, :] = ubuf[tm:tm + HIST_PAD, :]


def _pre_kernel(*refs, sample, tm):
    (x_ref, sh1_ref, sc1_ref, norm1_ref, win_ref, qan_ref, wq_ref, kvan_ref, cos_ref, sin_ref, qg_ref, kg_ref,
     *rest) = refs
    if sample:
        wkt_ref, qabs_ref, qr_ref, lat_ref, kr_ref, u_ref = rest
    else:
        (wkv_ref, wpool_ref, bpool_ref, pscale_ref,
         q_ref, k_ref, v_ref, lat_ref, kr_ref, pool_o_ref, utail_ref, ubuf) = rest

    x = x_ref[...]
    h = _rms(x, norm1_ref[...]) * (1.0 + _rows(sc1_ref[...], tm)) + _rows(sh1_ref[...], tm)
    z = jnp.dot(h.astype(BF16), win_ref[...], preferred_element_type=F32)
    o_kv = Q_LORA
    o_u = o_kv + KV_LORA
    o_kr = z.shape[1] - LANES
    cq = z[:, 0:o_kv]
    ckv = z[:, o_kv:o_u]
    u = z[:, o_u:o_kr]
    cosx = cos_ref[...]
    sinx = sin_ref[...]

    lat = _rms(ckv, kvan_ref[...])
    lat_ref[...] = lat
    kr = _rope(z[:, o_kr:], cosx, sinx)
    kr_ref[...] = kr[:, 0:QK_ROPE]

    q = jnp.dot(_rms(cq, qan_ref[...]).astype(BF16), wq_ref[...], preferred_element_type=F32)
    qg = qg_ref[...]
    kg = kg_ref[...]
    if not sample:
        kv = jnp.dot(lat.astype(BF16), wkv_ref[...], preferred_element_type=F32)
        kr_ss = jnp.sum(kr * kr, axis=-1, keepdims=True)

    for hd in range(N_HEADS):
        c0 = hd * HEAD_PAD
        qn = q[:, c0:c0 + QK_NOPE]
        qr = _rope(q[:, c0 + QK_NOPE:c0 + HEAD_PAD], cosx, sinx)
        ss = jnp.sum(qn * qn, axis=-1, keepdims=True) + jnp.sum(qr * qr, axis=-1, keepdims=True)
        r = lax.rsqrt(ss * (1.0 / QK_HEAD) + EPS)
        qn = qn * r * qg[:, 0:QK_NOPE]
        qr = qr * r * qg[:, QK_NOPE:]
        if sample:
            qk = (qn * kg[:, 0:QK_NOPE]).astype(BF16)
            qabs = jnp.dot(qk, wkt_ref[hd * QK_NOPE:(hd + 1) * QK_NOPE, :], preferred_element_type=F32)
            qabs_ref[hd] = qabs.astype(BF16)
            qr_ref[hd] = (qr * kg[:, QK_NOPE:]).astype(BF16)
        else:
            q_ref[hd] = jnp.concatenate([qn, qr], axis=-1).astype(BF16)
            kn = kv[:, c0:c0 + QK_NOPE]
            rk = lax.rsqrt((jnp.sum(kn * kn, axis=-1, keepdims=True) + kr_ss) * (1.0 / QK_HEAD) + EPS)
            k_ref[hd] = jnp.concatenate([kn * rk * kg[:, 0:QK_NOPE], kr * rk * kg[:, QK_NOPE:]], axis=-1).astype(BF16)
            v_ref[hd] = kv[:, c0 + QK_NOPE:c0 + HEAD_PAD].astype(BF16)

    if sample:
        u_ref[...] = u
    else:
        s_idx = pl.program_id(1)
        _pool_prompt(u, ubuf, s_idx, tm, wpool_ref, bpool_ref, pscale_ref, pool_o_ref)

        @pl.when(s_idx == pl.num_programs(1) - 1)
        def _():
            utail_ref[...] = u[tm - HIST_PAD:tm, :]


def _const_spec(a):
    nd = a.ndim
    return pl.BlockSpec(a.shape, lambda *_: (0,) * nd)


def _pre_prompt(x, mod_p, cosx, sinx, w):
    b, s, d = x.shape
    tm = PROMPT_TILE
    pw = w["pool_scale"].shape[1]
    consts_a = [w["norm1"], w["w_in"], w["q_a_norm"], w["w_q"], w["kv_a_norm"]]
    consts_b = [w["q_gain"], w["k_gain"], w["w_kv"], w["w_pool"], w["b_pool"], w["pool_scale"]]
    tok = lambda width: pl.BlockSpec((None, tm, width), lambda i, j: (i, j, 0))
    head = lambda width: pl.BlockSpec((None, N_HEADS, tm, width), lambda i, j: (i, 0, j, 0))
    mod = lambda col: pl.BlockSpec((None, 1, d), lambda i, j: (i, 0, col))
    rope = pl.BlockSpec((tm, LANES), lambda i, j: (j, 0))
    return pl.pallas_call(
        functools.partial(_pre_kernel, sample=False, tm=tm),
        out_shape=(
            jax.ShapeDtypeStruct((b, N_HEADS, s, HEAD_PAD), BF16),
            jax.ShapeDtypeStruct((b, N_HEADS, s, HEAD_PAD), BF16),
            jax.ShapeDtypeStruct((b, N_HEADS, s, V_HEAD), BF16),
            jax.ShapeDtypeStruct((b, s, KV_LORA), F32),
            jax.ShapeDtypeStruct((b, s, QK_ROPE), F32),
            jax.ShapeDtypeStruct((b, s, pw), BF16),
            jax.ShapeDtypeStruct((b, HIST_PAD, pw), F32),
        ),
        grid=(b, s // tm),
        in_specs=[tok(d), mod(0), mod(1)] + [_const_spec(a) for a in consts_a] + [rope, rope]
        + [_const_spec(a) for a in consts_b],
        out_specs=(head(HEAD_PAD), head(HEAD_PAD), head(V_HEAD), tok(KV_LORA), tok(QK_ROPE), tok(pw),
                   pl.BlockSpec((None, HIST_PAD, pw), lambda i, j: (i, 0, 0))),
        scratch_shapes=[pltpu.VMEM((tm + HIST_PAD, pw), F32)],
        compiler_params=_params(("arbitrary", "arbitrary")),
        name="pre_prompt",
    )(x, mod_p, mod_p, *consts_a, cosx, sinx, *consts_b)


def _pre_sample(x, mod_s, cosx, sinx, w):
    n, d = x.shape
    bd = mod_s.shape[0]
    pw = w["pool_scale"].shape[1]
    consts = [w["norm1"], w["w_in"], w["q_a_norm"], w["w_q"], w["kv_a_norm"]]
    mod = lambda col: pl.BlockSpec((bd, d), lambda i: (0, col))
    full = lambda *shape: pl.BlockSpec(shape, lambda i: (0,) * len(shape))
    return pl.pallas_call(
        functools.partial(_pre_kernel, sample=True, tm=n),
        out_shape=(
            jax.ShapeDtypeStruct((N_HEADS, n, KV_LORA), BF16),
            jax.ShapeDtypeStruct((N_HEADS, n, LANES), BF16),
            jax.ShapeDtypeStruct((n, KV_LORA), F32),
            jax.ShapeDtypeStruct((n, QK_ROPE), F32),
            jax.ShapeDtypeStruct((n, pw), F32),
        ),
        grid=(1,),
        in_specs=[full(n, d), mod(0), mod(1)] + [_const_spec(a) for a in consts]
        + [full(n, LANES), full(n, LANES), _const_spec(w["q_gain"]), _const_spec(w["k_gain"]), _const_spec(w["w_kt"])],
        out_specs=(full(N_HEADS, n, KV_LORA), full(N_HEADS, n, LANES), full(n, KV_LORA), full(n, QK_ROPE),
                   full(n, pw)),
        compiler_params=_params(("arbitrary",)),
        name="pre_sample",
    )(x, mod_s, mod_s, *consts, cosx, sinx, w["q_gain"], w["k_gain"], w["w_kt"])


def _attn_kernel(q_ref, k_ref, v_ref, o_ref, *, seq, tile):
    n = seq // tile
    row = lax.broadcasted_iota(jnp.int32, (tile, tile), 0)
    col = lax.broadcasted_iota(jnp.int32, (tile, tile), 1)
    for qi in range(n):
        q = q_ref[qi * tile:(qi + 1) * tile, :]
        m = jnp.full((tile, 1), -jnp.inf, F32)
        l = jnp.zeros((tile, 1), F32)
        acc = jnp.zeros((tile, V_HEAD), F32)
        for ki in range(qi + 1):
            k = k_ref[ki * tile:(ki + 1) * tile, :]
            s = lax.dot_general(q, k, _NT, preferred_element_type=F32)
            if ki == qi:
                s = jnp.where(col <= row, s, -jnp.inf)
            m_new = jnp.maximum(m, jnp.max(s, axis=-1, keepdims=True))
            alpha = jnp.exp(m - m_new)
            p = jnp.exp(s - m_new)
            l = alpha * l + jnp.sum(p, axis=-1, keepdims=True)
            acc = alpha * acc + jnp.dot(p.astype(BF16), v_ref[ki * tile:(ki + 1) * tile, :],
                                        preferred_element_type=F32)
            m = m_new
        o_ref[qi * tile:(qi + 1) * tile, :] = (acc / l).astype(BF16)


def _attention_prompt(q, k, v):
    b, h, s, _ = q.shape
    head = lambda width: pl.BlockSpec((None, None, s, width), lambda i, j: (i, j, 0, 0))
    return pl.pallas_call(
        functools.partial(_attn_kernel, seq=s, tile=min(ATTN_TILE, s)),
        out_shape=jax.ShapeDtypeStruct((b, s, h * V_HEAD), BF16),
        grid=(b, h),
        in_specs=[head(HEAD_PAD), head(HEAD_PAD), head(V_HEAD)],
        out_specs=pl.BlockSpec((None, s, V_HEAD), lambda i, j: (i, 0, j)),
        compiler_params=_params(("arbitrary", "arbitrary")),
        name="attention_prompt",
    )(q, k, v)


def _sattn_kernel(pt_ref, q_ref, qr_ref, latn_ref, krn_ref, wkt_ref, clat_hbm, ckr_hbm, ctx_ref,
                  lat_buf, kr_buf, sem, lhs, s_buf, *, layer, n_seq, n_pages, page, n_new):
    b = pl.program_id(0)
    n_chunks = n_pages // CHUNK_PAGES
    total = n_seq * n_chunks
    g0 = b * n_chunks
    nq = q_ref.shape[0]
    n_k = wkt_ref.shape[0]

    def slot_of(g):
        return lax.rem(g, CACHE_SLOTS)

    def copies(g):
        slot = slot_of(g)
        out = []
        for p in range(CHUNK_PAGES):
            phys = pt_ref[g * CHUNK_PAGES + p]
            dst = pl.ds(p * page, page)
            out.append(pltpu.make_async_copy(clat_hbm.at[layer, phys], lat_buf.at[slot, dst, :], sem.at[0, slot]))
            out.append(pltpu.make_async_copy(ckr_hbm.at[layer, phys], kr_buf.at[slot, :, dst], sem.at[1, slot]))
        return out

    def start(g):
        for cp in copies(g):
            cp.start()

    def wait(g):
        for cp in copies(g):
            cp.wait()

    @pl.when(b == 0)
    def _():
        lhs[0:n_k, :] = wkt_ref[...]
        for g in range(min(PREFETCH_CHUNKS, total)):
            start(g)

    def fetch_ahead(g):
        @pl.when(g + PREFETCH_CHUNKS < total)
        def _():
            start(g + PREFETCH_CHUNKS)

    lhs[n_k:n_k + nq, :] = q_ref[...]

    def scores(lat, krt):
        t = lat.shape[0]
        big = lax.dot_general(lhs[...], lat.astype(BF16), _NT, preferred_element_type=F32)
        kn = big[0:n_k]
        ssq = jnp.sum((kn * kn).reshape(N_HEADS, QK_NOPE, t), axis=1)
        ssq = ssq + jnp.sum(krt * krt, axis=0, keepdims=True)
        r = lax.rsqrt(ssq * (1.0 / QK_HEAD) + EPS)
        sr = jnp.dot(qr_ref[:, 0:QK_ROPE], krt.astype(BF16), preferred_element_type=F32)
        return (big[n_k:] + sr) * jnp.tile(r, (nq // N_HEADS, 1))

    def update(s, lat, carry):
        m, l, acc = carry
        m_new = jnp.maximum(m, jnp.max(s, axis=-1, keepdims=True))
        alpha = jnp.exp(m - m_new)
        p = jnp.exp(s - m_new)
        l = alpha * l + jnp.sum(p, axis=-1, keepdims=True)
        acc = alpha * acc + jnp.dot(p.astype(BF16), lat.astype(BF16), preferred_element_type=F32)
        return m_new, l, acc

    wait(g0)
    fetch_ahead(g0)

    pad = LANES - latn_ref.shape[0]
    lat_new = jnp.concatenate([latn_ref[...], jnp.zeros((pad, KV_LORA), F32)], axis=0)
    key = lax.broadcasted_iota(jnp.int32, (nq, LANES), 1)
    tok = lax.broadcasted_iota(jnp.int32, (nq, LANES), 0) // N_HEADS
    s_new = jnp.where((key <= tok) & (key < n_new), scores(lat_new, krn_ref[...]), -jnp.inf)
    carry = (jnp.full((nq, 1), -jnp.inf, F32), jnp.zeros((nq, 1), F32), jnp.zeros((nq, KV_LORA), F32))
    carry = update(s_new, lat_new, carry)

    s_buf[0] = scores(lat_buf[slot_of(g0)], kr_buf[slot_of(g0)])

    def body(c, carry):
        g = g0 + c
        wait(g)
        fetch_ahead(g)
        par = lax.rem(c, 2)
        s_prev = s_buf[1 - par]
        s_buf[par] = scores(lat_buf[slot_of(g)], kr_buf[slot_of(g)])
        return update(s_prev, lat_buf[slot_of(g - 1)], carry)

    carry = lax.fori_loop(1, n_chunks, body, carry)
    m, l, acc = update(s_buf[(n_chunks - 1) % 2], lat_buf[slot_of(g0 + n_chunks - 1)], carry)
    ctx_ref[...] = acc / l


def _attention_sample(page_table, qabs, qr, lat_new, krt_new, w_kt, cache_lat, cache_krt, layer, n_new):
    bd, nq, _ = qabs.shape
    n_pages = page_table.shape[1]
    page = cache_lat.shape[2]
    assert n_pages % CHUNK_PAGES == 0 and n_pages // CHUNK_PAGES >= 2
    rows = CHUNK_PAGES * page
    n_k = w_kt.shape[0]
    per_seq = lambda r, c: pl.BlockSpec((None, r, c), lambda i, pt: (i, 0, 0))
    grid_spec = pltpu.PrefetchScalarGridSpec(
        num_scalar_prefetch=1,
        grid=(bd,),
        in_specs=[
            per_seq(nq, KV_LORA), per_seq(nq, LANES), per_seq(SUBLANES, KV_LORA), per_seq(QK_ROPE, LANES),
            pl.BlockSpec((n_k, KV_LORA), lambda i, pt: (0, 0)),
            pl.BlockSpec(memory_space=pl.ANY), pl.BlockSpec(memory_space=pl.ANY),
        ],
        out_specs=per_seq(nq, KV_LORA),
        scratch_shapes=[
            pltpu.VMEM((CACHE_SLOTS, rows, KV_LORA), F32),
            pltpu.VMEM((CACHE_SLOTS, QK_ROPE, rows), F32),
            pltpu.SemaphoreType.DMA((2, CACHE_SLOTS)),
            pltpu.VMEM((n_k + nq, KV_LORA), BF16),
            pltpu.VMEM((2, nq, rows), F32),
        ],
    )
    return pl.pallas_call(
        functools.partial(_sattn_kernel, layer=layer, n_seq=bd, n_pages=n_pages, page=page, n_new=n_new),
        out_shape=jax.ShapeDtypeStruct((bd, nq, KV_LORA), F32),
        grid_spec=grid_spec,
        compiler_params=_params(("arbitrary",)),
        name="attention_sample",
    )(page_table.reshape(-1), qabs, qr, lat_new, krt_new, w_kt, cache_lat, cache_krt)


def _ctx_to_heads_kernel(ctx_ref, wv_ref, o_ref):
    n_tok, n_head, bd, _ = ctx_ref.shape
    for t in range(n_tok):
        for hd in range(n_head):
            o = jnp.dot(ctx_ref[t, hd].astype(BF16), wv_ref[hd], preferred_element_type=F32)
            o_ref[t * bd:(t + 1) * bd, hd * V_HEAD:(hd + 1) * V_HEAD] = o.astype(BF16)


def _ctx_to_heads(ctx, w_v):
    n_tok, n_head, bd, _ = ctx.shape
    return pl.pallas_call(
        _ctx_to_heads_kernel,
        out_shape=jax.ShapeDtypeStruct((n_tok * bd, n_head * V_HEAD), BF16),
        grid=(1,),
        in_specs=[_const_spec(ctx), _const_spec(w_v)],
        out_specs=pl.BlockSpec((n_tok * bd, n_head * V_HEAD), lambda i: (0, 0)),
        compiler_params=_params(("arbitrary",)),
        name="ctx_to_heads",
    )(ctx, w_v)


def _pool_sample_kernel(u_ref, hist_ref, wpool_ref, bpool_ref, pscale_ref, o_ref, *, first_pos):
    n_tok, bd, pw = u_ref.shape
    n_hist = hist_ref.shape[0]
    pg = pw // len(POOL_WINDOWS)

    def row(i, c0):
        return hist_ref[i, :, c0:c0 + pg] if i < n_hist else u_ref[i - n_hist, :, c0:c0 + pg]

    for t in range(n_tok):
        for g, w in enumerate(POOL_WINDOWS):
            c0 = g * pg
            ug = u_ref[t, :, c0:c0 + pg]
            acc = ug
            for j in range(1, w):
                acc = acc + row(n_hist + t - j, c0)
            d = acc / float(min(first_pos + t + 1, w)) - ug
            o = jnp.dot(d.astype(BF16), wpool_ref[g], preferred_element_type=F32) + bpool_ref[:, c0:c0 + pg]
            o_ref[t * bd:(t + 1) * bd, c0:c0 + pg] = (o * pscale_ref[:, c0:c0 + pg]).astype(BF16)


def _pool_sample(u, hist, w, first_pos):
    n_tok, bd, pw = u.shape
    args = (u, hist, w["w_pool"], w["b_pool"], w["pool_scale"])
    return pl.pallas_call(
        functools.partial(_pool_sample_kernel, first_pos=first_pos),
        out_shape=jax.ShapeDtypeStruct((n_tok * bd, pw), BF16),
        grid=(1,),
        in_specs=[_const_spec(a) for a in args],
        out_specs=pl.BlockSpec((n_tok * bd, pw), lambda i: (0, 0)),
        compiler_params=_params(("arbitrary",)),
        name="pool_sample",
    )(*args)


def _mix_kernel(x_ref, a_ref, p_ref, wout_ref, g1_ref, sh2_ref, sc2_ref, norm2_ref, x1_ref, h2_ref):
    tm = x_ref.shape[0]
    aw = a_ref.shape[1]
    mix = jnp.dot(a_ref[...], wout_ref[0:aw, :], preferred_element_type=F32)
    mix = mix + jnp.dot(p_ref[...], wout_ref[aw:, :], preferred_element_type=F32)
    x1 = x_ref[...] + _rows(g1_ref[...], tm) * mix
    x1_ref[...] = x1
    h2 = _rms(x1, norm2_ref[...]) * (1.0 + _rows(sc2_ref[...], tm)) + _rows(sh2_ref[...], tm)
    h2_ref[...] = h2.astype(BF16)


def _ffn_body(h_ref, x1_ref, g2_ref, wg_ref, wu_ref, wd_ref, y_ref, *, axis):
    @pl.when(pl.program_id(axis) == 0)
    def _():
        y_ref[...] = x1_ref[...]

    h = h_ref[...]
    gate = jnp.dot(h, wg_ref[...], preferred_element_type=F32)
    up = jnp.dot(h, wu_ref[...], preferred_element_type=F32)
    act = (gate * jax.nn.sigmoid(gate) * up).astype(BF16)
    part = jnp.dot(act, wd_ref[...], preferred_element_type=F32)
    y_ref[...] += _rows(g2_ref[...], y_ref.shape[0]) * part


def _layer_out_prompt(x, attn_o, pool_o, mod_p, w):
    b, s, d = x.shape
    tm = PROMPT_TILE
    tok = lambda width: pl.BlockSpec((None, tm, width), lambda i, j: (i, j, 0))
    mod = lambda col: pl.BlockSpec((None, 1, d), lambda i, j: (i, 0, col))
    x1, h2 = pl.pallas_call(
        _mix_kernel,
        out_shape=(jax.ShapeDtypeStruct((b, s, d), F32), jax.ShapeDtypeStruct((b, s, d), BF16)),
        grid=(b, s // tm),
        in_specs=[tok(d), tok(attn_o.shape[2]), tok(pool_o.shape[2]), _const_spec(w["w_out"]),
                  mod(2), mod(3), mod(4), _const_spec(w["norm2"])],
        out_specs=(tok(d), tok(d)),
        compiler_params=_params(("arbitrary", "arbitrary")),
        name="mix_prompt",
    )(x, attn_o, pool_o, w["w_out"], mod_p, mod_p, mod_p, w["norm2"])

    tm, tf = FFN_TOKEN_TILE, FFN_HIDDEN_TILE
    dff = w["w_gate"].shape[1]
    tok3 = pl.BlockSpec((None, tm, d), lambda i, j, f: (i, j, 0))
    x1_spec = pl.BlockSpec((None, tm, d), lambda i, j, f: (i, j, 0), pipeline_mode=pl.Buffered(1))
    return pl.pallas_call(
        functools.partial(_ffn_body, axis=2),
        out_shape=jax.ShapeDtypeStruct((b, s, d), F32),
        grid=(b, s // tm, dff // tf),
        in_specs=[tok3, x1_spec, pl.BlockSpec((None, 1, d), lambda i, j, f: (i, 0, 5)),
                  pl.BlockSpec((d, tf), lambda i, j, f: (0, f)),
                  pl.BlockSpec((d, tf), lambda i, j, f: (0, f)),
                  pl.BlockSpec((tf, d), lambda i, j, f: (f, 0))],
        out_specs=tok3,
        compiler_params=_params(("arbitrary", "arbitrary", "arbitrary")),
        name="ffn_prompt",
    )(h2, x1, mod_p, w["w_gate"], w["w_up"], w["w_down"])


def _layer_out_sample(x, attn_o, pool_o, mod_s, w):
    n, d = x.shape
    bd = mod_s.shape[0]
    full = lambda *shape: pl.BlockSpec(shape, lambda *_: (0,) * len(shape))
    mod = lambda col: pl.BlockSpec((bd, d), lambda *_: (0, col))
    x1, h2 = pl.pallas_call(
        _mix_kernel,
        out_shape=(jax.ShapeDtypeStruct((n, d), F32), jax.ShapeDtypeStruct((n, d), BF16)),
        grid=(1,),
        in_specs=[full(n, d), full(*attn_o.shape), full(*pool_o.shape), _const_spec(w["w_out"]),
                  mod(2), mod(3), mod(4), _const_spec(w["norm2"])],
        out_specs=(full(n, d), full(n, d)),
        compiler_params=_params(("arbitrary",)),
        name="mix_sample",
    )(x, attn_o, pool_o, w["w_out"], mod_s, mod_s, mod_s, w["norm2"])

    tf = FFN_HIDDEN_TILE
    dff = w["w_gate"].shape[1]
    return pl.pallas_call(
        functools.partial(_ffn_body, axis=0),
        out_shape=jax.ShapeDtypeStruct((n, d), F32),
        grid=(dff // tf,),
        in_specs=[full(n, d), full(n, d), mod(5),
                  pl.BlockSpec((d, tf), lambda f: (0, f)),
                  pl.BlockSpec((d, tf), lambda f: (0, f)),
                  pl.BlockSpec((tf, d), lambda f: (f, 0))],
        out_specs=full(n, d),
        compiler_params=_params(("arbitrary",)),
        name="ffn_sample",
    )(h2, x1, mod_s, w["w_gate"], w["w_up"], w["w_down"])


def _rope_tables(pos):
    inv = 1.0 / (ROPE_THETA ** (jnp.arange(0, QK_ROPE, 2, dtype=F32) / QK_ROPE))
    ang = pos.astype(F32)[:, None] * inv[None, :]
    c, s = jnp.cos(ang), jnp.sin(ang)
    z = jnp.zeros((pos.shape[0], LANES - QK_ROPE), F32)
    return jnp.concatenate([c, c, z], axis=1), jnp.concatenate([-s, s, z], axis=1)


def _layer_weights(l, w_in, q_a_norm, w_q_b, kv_a_norm, w_kv_b, q_norm_nope, q_norm_rope, k_norm_nope, k_norm_rope,
                   w_pool, b_pool, pool_scale, w_out, norm1, norm2, w_gate, w_up, w_down):
    half = QK_ROPE // 2
    o_kr = Q_LORA + KV_LORA
    o_u = o_kr + QK_ROPE
    win = w_in[l]
    w_in_p = jnp.concatenate([win[:, :o_kr], win[:, o_u:], win[:, o_kr:o_u],
                              win[:, o_kr + half:o_u], win[:, o_kr:o_kr + half]], axis=1)
    wq = w_q_b[l].reshape(Q_LORA, N_HEADS, QK_HEAD)
    w_q_p = jnp.concatenate([wq, wq[:, :, QK_NOPE + half:], wq[:, :, QK_NOPE:QK_NOPE + half]], axis=2)
    wkv = w_kv_b[l].reshape(KV_LORA, N_HEADS, QK_NOPE + V_HEAD)
    zpad = jnp.zeros((HEAD_PAD - QK_HEAD,), F32)
    row = lambda a: a.reshape(1, -1)
    return {
        "norm1": row(norm1[l]), "norm2": row(norm2[l]),
        "w_in": w_in_p.astype(BF16),
        "q_a_norm": row(q_a_norm[l]), "kv_a_norm": row(kv_a_norm[l]),
        "w_q": w_q_p.reshape(Q_LORA, N_HEADS * HEAD_PAD).astype(BF16),
        "w_kv": w_kv_b[l].astype(BF16),
        "w_kt": wkv[:, :, :QK_NOPE].reshape(KV_LORA, N_HEADS * QK_NOPE).T.astype(BF16),
        "w_v": wkv[:, :, QK_NOPE:].transpose(1, 0, 2).astype(BF16),
        "q_gain": row(jnp.concatenate([q_norm_nope[l], q_norm_rope[l], q_norm_rope[l], zpad]) * SCALE),
        "k_gain": row(jnp.concatenate([k_norm_nope[l], k_norm_rope[l], k_norm_rope[l], zpad])),
        "w_pool": w_pool[l].astype(BF16), "b_pool": row(b_pool[l]), "pool_scale": row(pool_scale[l]),
        "w_out": w_out[l].astype(BF16),
        "w_gate": w_gate[l].astype(BF16), "w_up": w_up[l].astype(BF16), "w_down": w_down[l].astype(BF16),
    }


def kernel(x_prompt, x_sample, cache_kv_latent, cache_k_rope, state_pool, page_table, c_prompt, c_sample,
           w_mod, b_mod, norm1, w_in, q_a_norm, w_q_b, kv_a_norm, w_kv_b,
           q_norm_nope, q_norm_rope, k_norm_nope, k_norm_rope,
           w_pool, b_pool, pool_scale, w_out, norm2, w_gate, w_up, w_down):
    b, s, d = x_prompt.shape
    bd, t, _ = x_sample.shape
    depth = w_mod.shape[0]
    past = page_table.shape[1] * cache_kv_latent.shape[2]
    cos_p, sin_p = _rope_tables(jnp.arange(s))
    cos_s, sin_s = _rope_tables(jnp.repeat(past + jnp.arange(t), bd))
    c_all = jnp.concatenate([c_prompt, c_sample], axis=0)
    cache_krt = jnp.swapaxes(cache_k_rope, 2, 3)

    yp = x_prompt
    ys = x_sample.transpose(1, 0, 2).reshape(t * bd, d)
    outs = [[] for _ in range(6)]
    for l in range(depth):
        w = _layer_weights(l, w_in, q_a_norm, w_q_b, kv_a_norm, w_kv_b, q_norm_nope, q_norm_rope, k_norm_nope,
                           k_norm_rope, w_pool, b_pool, pool_scale, w_out, norm1, norm2, w_gate, w_up, w_down)
        mod = _modulation(c_all, w_mod[l], b_mod[l].reshape(1, -1))
        mod_p = mod[:b].reshape(b, 1, -1)
        mod_s = mod[b:]

        q, k, v, lat_p, kr_p, pool_p, utail = _pre_prompt(yp, mod_p, cos_p, sin_p, w)
        attn_p = _attention_prompt(q, k, v)
        yp = _layer_out_prompt(yp, attn_p, pool_p, mod_p, w)

        qabs, qr, lat_s, kr_s, u_s = _pre_sample(ys, mod_s, cos_s, sin_s, w)
        per_seq = lambda a: a.reshape(N_HEADS, t, bd, -1).transpose(2, 1, 0, 3).reshape(bd, t * N_HEADS, -1)
        lat_s = lat_s.reshape(t, bd, -1).transpose(1, 0, 2)
        kr_s = kr_s.reshape(t, bd, -1).transpose(1, 0, 2)
        lat_new = jnp.pad(lat_s, ((0, 0), (0, SUBLANES - t), (0, 0)))
        krt_new = jnp.pad(kr_s.transpose(0, 2, 1), ((0, 0), (0, 0), (0, LANES - t)))
        ctx = _attention_sample(page_table, per_seq(qabs), per_seq(qr), lat_new, krt_new, w["w_kt"],
                                cache_kv_latent, cache_krt, l, t)
        attn_s = _ctx_to_heads(ctx.reshape(bd, t, N_HEADS, -1).transpose(1, 2, 0, 3), w["w_v"])
        u_s = u_s.reshape(t, bd, -1)
        pool_s = _pool_sample(u_s, state_pool[l].transpose(1, 0, 2), w, past)
        ys = _layer_out_sample(ys, attn_s, pool_s, mod_s, w)

        full_s = jnp.concatenate([state_pool[l], u_s.transpose(1, 0, 2)], axis=1)
        for dst, val in zip(outs, (lat_p, kr_p, utail[:, HIST_PAD - POOL_HIST:], lat_s, kr_s,
                                   full_s[:, -POOL_HIST:])):
            dst.append(val)

    ys = ys.reshape(t, bd, d).transpose(1, 0, 2)
    return (yp, ys) + tuple(jnp.stack(o) for o in outs)
```

```python
import functools

import jax
import jax.numpy as jnp
from jax import lax
from jax.experimental import pallas as pl
from jax.experimental.pallas import tpu as pltpu

F32 = jnp.float32
BF16 = jnp.bfloat16

N_HEADS = 8
QK_NOPE = 128
QK_ROPE = 64
QK_HEAD = QK_NOPE + QK_ROPE
V_HEAD = 128
Q_LORA = 512
KV_LORA = 256
POOL_WINDOWS = (2, 4, 8, 16)
POOL_HIST = max(POOL_WINDOWS) - 1
ROPE_THETA = 10000.0
EPS = 1e-6
SCALE = QK_HEAD ** -0.5

LANES = 128
SUBLANES = 8
HEAD_PAD = 2 * LANES
HIST_PAD = 2 * SUBLANES
VMEM_LIMIT_BYTES = 56 * 1024 * 1024

PROMPT_TILE = 512
ATTN_TILE = 512
FFN_TOKEN_TILE = 512
FFN_HIDDEN_TILE = 512
CHUNK_PAGES = 16
PREFETCH_CHUNKS = 3
CACHE_SLOTS = PREFETCH_CHUNKS + 2
SEQ_STREAMS = 2

_NT = (((1,), (1,)), ((), ()))


def _rms(x, w):
    return x * lax.rsqrt(jnp.mean(x * x, axis=-1, keepdims=True) + EPS) * w


def _rows(a, n):
    r = a.shape[0]
    if r == 1 or r == n:
        return a
    return jnp.tile(a, (n // r, 1))


def _rope(blk, cosx, sinx):
    return blk * cosx + pltpu.roll(blk, LANES // 2, axis=1) * sinx


def _params(semantics):
    return pltpu.CompilerParams(dimension_semantics=semantics, vmem_limit_bytes=VMEM_LIMIT_BYTES)


def _mod_kernel(c_ref, w_ref, b_ref, o_ref):
    c = c_ref[...]
    a = (c * jax.nn.sigmoid(c)).astype(BF16)
    o_ref[...] = jnp.dot(a, w_ref[...].astype(BF16), preferred_element_type=F32) + b_ref[...]


def _modulation(c_all, w_mod, b_mod):
    n, d = c_all.shape
    width = w_mod.shape[1]
    tn = 1024
    return pl.pallas_call(
        _mod_kernel,
        out_shape=jax.ShapeDtypeStruct((n, width), F32),
        grid=(width // tn,),
        in_specs=[
            pl.BlockSpec((n, d), lambda j: (0, 0)),
            pl.BlockSpec((d, tn), lambda j: (0, j)),
            pl.BlockSpec((1, tn), lambda j: (0, j)),
        ],
        out_specs=pl.BlockSpec((n, tn), lambda j: (0, j)),
        compiler_params=_params(("arbitrary",)),
        name="modulation",
    )(c_all, w_mod, b_mod)


def _pool_prompt(u, ubuf, s_idx, tm, wpool_ref, bpool_ref, pscale_ref, pool_o_ref):
    pw = u.shape[1]
    pg = pw // len(POOL_WINDOWS)
    ubuf[HIST_PAD:HIST_PAD + tm, :] = u
    pos = s_idx * tm + lax.broadcasted_iota(jnp.int32, (tm, 1), 0)
    for g, w in enumerate(POOL_WINDOWS):
        c0 = g * pg
        ug = u[:, c0:c0 + pg]
        acc = ug
        for j in range(1, w):
            acc = acc + ubuf[HIST_PAD - j:HIST_PAD - j + tm, c0:c0 + pg]
        cnt = jnp.minimum(pos + 1, w).astype(F32)
        d = acc / cnt - ug
        o = jnp.dot(d.astype(BF16), wpool_ref[g], preferred_element_type=F32) + bpool_ref[:, c0:c0 + pg]
        pool_o_ref[:, c0:c0 + pg] = (o * pscale_ref[:, c0:c0 + pg]).astype(BF16)
    ubuf[0:HIST_PAD, :] = ubuf[tm:tm + HIST_PAD, :]


def _pre_kernel(*refs, sample, tm):
    (x_ref, sh1_ref, sc1_ref, norm1_ref, win_ref, qan_ref, wq_ref, kvan_ref, cos_ref, sin_ref, qg_ref, kg_ref,
     *rest) = refs
    if sample:
        wkt_ref, qabs_ref, qr_ref, lat_ref, kr_ref, u_ref = rest
    else:
        (wkv_ref, wpool_ref, bpool_ref, pscale_ref,
         q_ref, k_ref, v_ref, lat_ref, kr_ref, pool_o_ref, utail_ref, ubuf) = rest
        s_idx = pl.program_id(1)

        @pl.when(s_idx == 0)
        def _():
            ubuf[0:HIST_PAD, :] = jnp.zeros((HIST_PAD, ubuf.shape[1]), F32)

    o_kv = Q_LORA
    o_kr = o_kv + KV_LORA
    o_u = o_kr + LANES
    x = x_ref[...]
    h = _rms(x, norm1_ref[...]) * (1.0 + _rows(sc1_ref[...], tm)) + _rows(sh1_ref[...], tm)
    h = h.astype(BF16)
    cosx = cos_ref[...]
    sinx = sin_ref[...]
    qg = qg_ref[...]
    kg = kg_ref[...]

    u = jnp.dot(h, win_ref[:, o_u:], preferred_element_type=F32)
    if sample:
        u_ref[...] = u
    else:
        _pool_prompt(u, ubuf, s_idx, tm, wpool_ref, bpool_ref, pscale_ref, pool_o_ref)

    cq = jnp.dot(h, win_ref[:, 0:o_kv], preferred_element_type=F32)
    q = jnp.dot(_rms(cq, qan_ref[...]).astype(BF16), wq_ref[...], preferred_element_type=F32)
    for hd in range(N_HEADS):
        c0 = hd * HEAD_PAD
        qn = q[:, c0:c0 + QK_NOPE]
        qr = _rope(q[:, c0 + QK_NOPE:c0 + HEAD_PAD], cosx, sinx)
        r = lax.rsqrt(jnp.sum(qn * qn + qr * qr, axis=-1, keepdims=True) * (1.0 / QK_HEAD) + EPS)
        qn = qn * r * qg[:, 0:QK_NOPE]
        qr = qr * r * qg[:, QK_NOPE:]
        if sample:
            qk = (qn * kg[:, 0:QK_NOPE]).astype(BF16)
            qabs = jnp.dot(qk, wkt_ref[hd * QK_NOPE:(hd + 1) * QK_NOPE, :], preferred_element_type=F32)
            qabs_ref[hd] = qabs.astype(BF16)
            qr_ref[hd] = (qr * kg[:, QK_NOPE:]).astype(BF16)
        else:
            q_ref[hd] = jnp.concatenate([qn, qr], axis=-1).astype(BF16)

    zk = jnp.dot(h, win_ref[:, o_kv:o_u], preferred_element_type=F32)
    lat = _rms(zk[:, 0:KV_LORA], kvan_ref[...])
    lat_ref[...] = lat
    kr = _rope(zk[:, KV_LORA:], cosx, sinx)
    kr_ref[...] = kr[:, 0:QK_ROPE]
    if not sample:
        kv = jnp.dot(lat.astype(BF16), wkv_ref[...], preferred_element_type=F32)
        kr_sq = kr * kr
        for hd in range(N_HEADS):
            c0 = hd * HEAD_PAD
            kn = kv[:, c0:c0 + QK_NOPE]
            rk = lax.rsqrt(jnp.sum(kn * kn + kr_sq, axis=-1, keepdims=True) * (1.0 / QK_HEAD) + EPS)
            k_ref[hd] = jnp.concatenate([kn * rk * kg[:, 0:QK_NOPE], kr * rk * kg[:, QK_NOPE:]], axis=-1).astype(BF16)
            v_ref[hd] = kv[:, c0 + QK_NOPE:c0 + HEAD_PAD].astype(BF16)

        @pl.when(s_idx == pl.num_programs(1) - 1)
        def _():
            utail_ref[...] = ubuf[0:HIST_PAD, :]


def _const_spec(a):
    nd = a.ndim
    return pl.BlockSpec(a.shape, lambda *_: (0,) * nd)


def _pre_prompt(x, mod_p, cosx, sinx, w):
    b, s, d = x.shape
    tm = PROMPT_TILE
    pw = w["pool_scale"].shape[1]
    consts_a = [w["norm1"], w["w_in"], w["q_a_norm"], w["w_q"], w["kv_a_norm"]]
    consts_b = [w["q_gain"], w["k_gain"], w["w_kv"], w["w_pool"], w["b_pool"], w["pool_scale"]]
    tok = lambda width: pl.BlockSpec((None, tm, width), lambda i, j: (i, j, 0))
    head = lambda width: pl.BlockSpec((None, N_HEADS, tm, width), lambda i, j: (i, 0, j, 0))
    mod = lambda col: pl.BlockSpec((None, 1, d), lambda i, j: (i, 0, col))
    rope = pl.BlockSpec((tm, LANES), lambda i, j: (j, 0))
    return pl.pallas_call(
        functools.partial(_pre_kernel, sample=False, tm=tm),
        out_shape=(
            jax.ShapeDtypeStruct((b, N_HEADS, s, HEAD_PAD), BF16),
            jax.ShapeDtypeStruct((b, N_HEADS, s, HEAD_PAD), BF16),
            jax.ShapeDtypeStruct((b, N_HEADS, s, V_HEAD), BF16),
            jax.ShapeDtypeStruct((b, s, KV_LORA), F32),
            jax.ShapeDtypeStruct((b, s, QK_ROPE), F32),
            jax.ShapeDtypeStruct((b, s, pw), BF16),
            jax.ShapeDtypeStruct((b, HIST_PAD, pw), F32),
        ),
        grid=(b, s // tm),
        in_specs=[tok(d), mod(0), mod(1)] + [_const_spec(a) for a in consts_a] + [rope, rope]
        + [_const_spec(a) for a in consts_b],
        out_specs=(head(HEAD_PAD), head(HEAD_PAD), head(V_HEAD), tok(KV_LORA), tok(QK_ROPE), tok(pw),
                   pl.BlockSpec((None, HIST_PAD, pw), lambda i, j: (i, 0, 0))),
        scratch_shapes=[pltpu.VMEM((tm + HIST_PAD, pw), F32)],
        compiler_params=_params(("arbitrary", "arbitrary")),
        name="pre_prompt",
    )(x, mod_p, mod_p, *consts_a, cosx, sinx, *consts_b)


def _pre_sample(x, mod_s, cosx, sinx, w):
    n, d = x.shape
    bd = mod_s.shape[0]
    pw = w["pool_scale"].shape[1]
    consts = [w["norm1"], w["w_in"], w["q_a_norm"], w["w_q"], w["kv_a_norm"]]
    mod = lambda col: pl.BlockSpec((bd, d), lambda i: (0, col))
    full = lambda *shape: pl.BlockSpec(shape, lambda i: (0,) * len(shape))
    return pl.pallas_call(
        functools.partial(_pre_kernel, sample=True, tm=n),
        out_shape=(
            jax.ShapeDtypeStruct((N_HEADS, n, KV_LORA), BF16),
            jax.ShapeDtypeStruct((N_HEADS, n, LANES), BF16),
            jax.ShapeDtypeStruct((n, KV_LORA), F32),
            jax.ShapeDtypeStruct((n, QK_ROPE), F32),
            jax.ShapeDtypeStruct((n, pw), F32),
        ),
        grid=(1,),
        in_specs=[full(n, d), mod(0), mod(1)] + [_const_spec(a) for a in consts]
        + [full(n, LANES), full(n, LANES), _const_spec(w["q_gain"]), _const_spec(w["k_gain"]), _const_spec(w["w_kt"])],
        out_specs=(full(N_HEADS, n, KV_LORA), full(N_HEADS, n, LANES), full(n, KV_LORA), full(n, QK_ROPE),
                   full(n, pw)),
        compiler_params=_params(("arbitrary",)),
        name="pre_sample",
    )(x, mod_s, mod_s, *consts, cosx, sinx, w["q_gain"], w["k_gain"], w["w_kt"])


def _attn_kernel(q_ref, k_ref, v_ref, o_ref, *, seq, tile):
    n = seq // tile
    row = lax.broadcasted_iota(jnp.int32, (tile, tile), 0)
    col = lax.broadcasted_iota(jnp.int32, (tile, tile), 1)
    for qi in range(n):
        q = q_ref[qi * tile:(qi + 1) * tile, :]
        m = jnp.full((tile, 1), -jnp.inf, F32)
        l = jnp.zeros((tile, 1), F32)
        acc = jnp.zeros((tile, V_HEAD), F32)
        for ki in range(qi + 1):
            k = k_ref[ki * tile:(ki + 1) * tile, :]
            s = lax.dot_general(q, k, _NT, preferred_element_type=F32)
            if ki == qi:
                s = jnp.where(col <= row, s, -jnp.inf)
            m_new = jnp.maximum(m, jnp.max(s, axis=-1, keepdims=True))
            alpha = jnp.exp(m - m_new)
            p = jnp.exp(s - m_new)
            l = alpha * l + jnp.sum(p, axis=-1, keepdims=True)
            acc = alpha * acc + jnp.dot(p.astype(BF16), v_ref[ki * tile:(ki + 1) * tile, :],
                                        preferred_element_type=F32)
            m = m_new
        o_ref[qi * tile:(qi + 1) * tile, :] = (acc / l).astype(BF16)


def _attention_prompt(q, k, v):
    b, h, s, _ = q.shape
    head = lambda width: pl.BlockSpec((None, None, s, width), lambda i, j: (i, j, 0, 0))
    return pl.pallas_call(
        functools.partial(_attn_kernel, seq=s, tile=min(ATTN_TILE, s)),
        out_shape=jax.ShapeDtypeStruct((b, s, h * V_HEAD), BF16),
        grid=(b, h),
        in_specs=[head(HEAD_PAD), head(HEAD_PAD), head(V_HEAD)],
        out_specs=pl.BlockSpec((None, s, V_HEAD), lambda i, j: (i, 0, j)),
        compiler_params=_params(("arbitrary", "arbitrary")),
        name="attention_prompt",
    )(q, k, v)


def _sattn_kernel(pt_ref, q_ref, qr_ref, latn_ref, krn_ref, wkt_ref, clat_hbm, ckr_hbm, ctx_ref,
                  lat_buf, kr_buf, sem, lhs, s_even, s_odd, *, layer, n_seq, n_pages, page, n_new):
    b = pl.program_id(0)
    n_chunks = n_pages // CHUNK_PAGES
    per_stream = (n_seq // SEQ_STREAMS) * n_chunks
    g0 = b * n_chunks
    streams = range(SEQ_STREAMS)
    nq = q_ref.shape[1]
    n_k = wkt_ref.shape[0]
    s_bufs = (s_even, s_odd)

    def slot_of(g):
        return lax.rem(g, CACHE_SLOTS)

    def copies(st, g):
        slot = slot_of(g)
        out = []
        for p in range(CHUNK_PAGES):
            phys = pt_ref[(st * per_stream + g) * CHUNK_PAGES + p]
            dst = pl.ds(p * page, page)
            out.append(pltpu.make_async_copy(clat_hbm.at[layer, phys], lat_buf.at[st, slot, dst, :],
                                             sem.at[st, 0, slot]))
            out.append(pltpu.make_async_copy(ckr_hbm.at[layer, phys], kr_buf.at[st, slot, :, dst],
                                             sem.at[st, 1, slot]))
        return out

    def start(st, g):
        for cp in copies(st, g):
            cp.start()

    def wait(st, g):
        for cp in copies(st, g):
            cp.wait()

    def fetch_ahead(st, g):
        @pl.when(g + PREFETCH_CHUNKS < per_stream)
        def _():
            start(st, g + PREFETCH_CHUNKS)

    @pl.when(b == 0)
    def _():
        for st in streams:
            lhs[st, 0:n_k, :] = wkt_ref[...]
            for g in range(min(PREFETCH_CHUNKS, per_stream)):
                start(st, g)

    for st in streams:
        lhs[st, n_k:n_k + nq, :] = q_ref[st]

    def scores(st, lat, krt):
        t = lat.shape[0]
        big = lax.dot_general(lhs[st], lat.astype(BF16), _NT, preferred_element_type=F32)
        kn = big[0:n_k]
        ssq = jnp.sum((kn * kn).reshape(N_HEADS, QK_NOPE, t), axis=1)
        ssq = ssq + jnp.sum(krt * krt, axis=0, keepdims=True)
        r = lax.rsqrt(ssq * (1.0 / QK_HEAD) + EPS)
        sr = jnp.dot(qr_ref[st, :, 0:QK_ROPE], krt.astype(BF16), preferred_element_type=F32)
        return (big[n_k:] + sr) * jnp.tile(r, (nq // N_HEADS, 1))

    def update(s, lat, carry):
        m, l, acc = carry
        m_new = jnp.maximum(m, jnp.max(s, axis=-1, keepdims=True))
        alpha = jnp.exp(m - m_new)
        p = jnp.exp(s - m_new)
        l = alpha * l + jnp.sum(p, axis=-1, keepdims=True)
        acc = alpha * acc + jnp.dot(p.astype(BF16), lat.astype(BF16), preferred_element_type=F32)
        return m_new, l, acc

    for st in streams:
        wait(st, g0)
        fetch_ahead(st, g0)

    pad = LANES - latn_ref.shape[1]
    key = lax.broadcasted_iota(jnp.int32, (nq, LANES), 1)
    tok = lax.broadcasted_iota(jnp.int32, (nq, LANES), 0) // N_HEADS
    carries = []
    for st in streams:
        lat_new = jnp.concatenate([latn_ref[st], jnp.zeros((pad, KV_LORA), F32)], axis=0)
        s_new = jnp.where((key <= tok) & (key < n_new), scores(st, lat_new, krn_ref[st]), -jnp.inf)
        carry = (jnp.full((nq, 1), -jnp.inf, F32), jnp.zeros((nq, 1), F32), jnp.zeros((nq, KV_LORA), F32))
        carries.append(update(s_new, lat_new, carry))

    for st in streams:
        s_even[st] = scores(st, lat_buf[st, slot_of(g0)], kr_buf[st, slot_of(g0)])

    def step(c, parity, carries):
        g = g0 + c
        for st in streams:
            wait(st, g)
            fetch_ahead(st, g)
        out = []
        for st in streams:
            s_bufs[parity][st] = scores(st, lat_buf[st, slot_of(g)], kr_buf[st, slot_of(g)])
            out.append(update(s_bufs[1 - parity][st], lat_buf[st, slot_of(g - 1)], carries[st]))
        return tuple(out)

    def two_steps(i, carries):
        c = 1 + 2 * i
        return step(c + 1, 0, step(c, 1, carries))

    carries = lax.fori_loop(0, (n_chunks - 1) // 2, two_steps, tuple(carries))
    if n_chunks % 2 == 0:
        carries = step(n_chunks - 1, 1, carries)
    for st in streams:
        m, l, acc = update(s_bufs[(n_chunks - 1) % 2][st], lat_buf[st, slot_of(g0 + n_chunks - 1)], carries[st])
        ctx_ref[st] = acc / l


def _attention_sample(page_table, qabs, qr, lat_new, krt_new, w_kt, cache_lat, cache_krt, layer, n_new):
    bd, nq, _ = qabs.shape
    n_pages = page_table.shape[1]
    page = cache_lat.shape[2]
    assert n_pages % CHUNK_PAGES == 0 and n_pages // CHUNK_PAGES >= 2 and bd % SEQ_STREAMS == 0
    rows = CHUNK_PAGES * page
    n_k = w_kt.shape[0]
    pt = page_table.reshape(bd // SEQ_STREAMS, SEQ_STREAMS, n_pages).transpose(1, 0, 2).reshape(-1)
    per_step = lambda r, c: pl.BlockSpec((SEQ_STREAMS, r, c), lambda i, pt: (i, 0, 0))
    grid_spec = pltpu.PrefetchScalarGridSpec(
        num_scalar_prefetch=1,
        grid=(bd // SEQ_STREAMS,),
        in_specs=[
            per_step(nq, KV_LORA), per_step(nq, LANES), per_step(SUBLANES, KV_LORA), per_step(QK_ROPE, LANES),
            pl.BlockSpec((n_k, KV_LORA), lambda i, pt: (0, 0)),
            pl.BlockSpec(memory_space=pl.ANY), pl.BlockSpec(memory_space=pl.ANY),
        ],
        out_specs=per_step(nq, KV_LORA),
        scratch_shapes=[
            pltpu.VMEM((SEQ_STREAMS, CACHE_SLOTS, rows, KV_LORA), F32),
            pltpu.VMEM((SEQ_STREAMS, CACHE_SLOTS, QK_ROPE, rows), F32),
            pltpu.SemaphoreType.DMA((SEQ_STREAMS, 2, CACHE_SLOTS)),
            pltpu.VMEM((SEQ_STREAMS, n_k + nq, KV_LORA), BF16),
            pltpu.VMEM((SEQ_STREAMS, nq, rows), F32),
            pltpu.VMEM((SEQ_STREAMS, nq, rows), F32),
        ],
    )
    return pl.pallas_call(
        functools.partial(_sattn_kernel, layer=layer, n_seq=bd, n_pages=n_pages, page=page, n_new=n_new),
        out_shape=jax.ShapeDtypeStruct((bd, nq, KV_LORA), F32),
        grid_spec=grid_spec,
        compiler_params=_params(("arbitrary",)),
        name="attention_sample",
    )(pt, qabs, qr, lat_new, krt_new, w_kt, cache_lat, cache_krt)


def _ctx_to_heads_kernel(ctx_ref, wv_ref, o_ref):
    n_tok, n_head, bd, _ = ctx_ref.shape
    for t in range(n_tok):
        for hd in range(n_head):
            o = jnp.dot(ctx_ref[t, hd].astype(BF16), wv_ref[hd], preferred_element_type=F32)
            o_ref[t * bd:(t + 1) * bd, hd * V_HEAD:(hd + 1) * V_HEAD] = o.astype(BF16)


def _ctx_to_heads(ctx, w_v):
    n_tok, n_head, bd, _ = ctx.shape
    return pl.pallas_call(
        _ctx_to_heads_kernel,
        out_shape=jax.ShapeDtypeStruct((n_tok * bd, n_head * V_HEAD), BF16),
        grid=(1,),
        in_specs=[_const_spec(ctx), _const_spec(w_v)],
        out_specs=pl.BlockSpec((n_tok * bd, n_head * V_HEAD), lambda i: (0, 0)),
        compiler_params=_params(("arbitrary",)),
        name="ctx_to_heads",
    )(ctx, w_v)


def _pool_sample_kernel(u_ref, hist_ref, wpool_ref, bpool_ref, pscale_ref, o_ref, *, first_pos):
    n_tok, bd, pw = u_ref.shape
    n_hist = hist_ref.shape[0]
    pg = pw // len(POOL_WINDOWS)

    def row(i, c0):
        return hist_ref[i, :, c0:c0 + pg] if i < n_hist else u_ref[i - n_hist, :, c0:c0 + pg]

    for t in range(n_tok):
        for g, w in enumerate(POOL_WINDOWS):
            c0 = g * pg
            ug = u_ref[t, :, c0:c0 + pg]
            acc = ug
            for j in range(1, w):
                acc = acc + row(n_hist + t - j, c0)
            d = acc / float(min(first_pos + t + 1, w)) - ug
            o = jnp.dot(d.astype(BF16), wpool_ref[g], preferred_element_type=F32) + bpool_ref[:, c0:c0 + pg]
            o_ref[t * bd:(t + 1) * bd, c0:c0 + pg] = (o * pscale_ref[:, c0:c0 + pg]).astype(BF16)


def _pool_sample(u, hist, w, first_pos):
    n_tok, bd, pw = u.shape
    args = (u, hist, w["w_pool"], w["b_pool"], w["pool_scale"])
    return pl.pallas_call(
        functools.partial(_pool_sample_kernel, first_pos=first_pos),
        out_shape=jax.ShapeDtypeStruct((n_tok * bd, pw), BF16),
        grid=(1,),
        in_specs=[_const_spec(a) for a in args],
        out_specs=pl.BlockSpec((n_tok * bd, pw), lambda i: (0, 0)),
        compiler_params=_params(("arbitrary",)),
        name="pool_sample",
    )(*args)


def _mix_kernel(x_ref, a_ref, p_ref, wout_ref, g1_ref, sh2_ref, sc2_ref, norm2_ref, x1_ref, h2_ref):
    tm = x_ref.shape[0]
    aw = a_ref.shape[1]
    mix = jnp.dot(a_ref[...], wout_ref[0:aw, :], preferred_element_type=F32)
    mix = mix + jnp.dot(p_ref[...], wout_ref[aw:, :], preferred_element_type=F32)
    x1 = x_ref[...] + _rows(g1_ref[...], tm) * mix
    x1_ref[...] = x1
    h2 = _rms(x1, norm2_ref[...]) * (1.0 + _rows(sc2_ref[...], tm)) + _rows(sh2_ref[...], tm)
    h2_ref[...] = h2.astype(BF16)


def _ffn_body(h_ref, x1_ref, g2_ref, wg_ref, wu_ref, wd_ref, y_ref, *, axis):
    @pl.when(pl.program_id(axis) == 0)
    def _():
        y_ref[...] = x1_ref[...]

    h = h_ref[...]
    gate = jnp.dot(h, wg_ref[...], preferred_element_type=F32)
    up = jnp.dot(h, wu_ref[...], preferred_element_type=F32)
    act = (gate * jax.nn.sigmoid(gate) * up).astype(BF16)
    part = jnp.dot(act, wd_ref[...], preferred_element_type=F32)
    y_ref[...] += _rows(g2_ref[...], y_ref.shape[0]) * part


def _layer_out_prompt(x, attn_o, pool_o, mod_p, w):
    b, s, d = x.shape
    tm = PROMPT_TILE
    tok = lambda width: pl.BlockSpec((None, tm, width), lambda i, j: (i, j, 0))
    mod = lambda col: pl.BlockSpec((None, 1, d), lambda i, j: (i, 0, col))
    x1, h2 = pl.pallas_call(
        _mix_kernel,
        out_shape=(jax.ShapeDtypeStruct((b, s, d), F32), jax.ShapeDtypeStruct((b, s, d), BF16)),
        grid=(b, s // tm),
        in_specs=[tok(d), tok(attn_o.shape[2]), tok(pool_o.shape[2]), _const_spec(w["w_out"]),
                  mod(2), mod(3), mod(4), _const_spec(w["norm2"])],
        out_specs=(tok(d), tok(d)),
        compiler_params=_params(("arbitrary", "arbitrary")),
        name="mix_prompt",
    )(x, attn_o, pool_o, w["w_out"], mod_p, mod_p, mod_p, w["norm2"])

    tm, tf = FFN_TOKEN_TILE, FFN_HIDDEN_TILE
    dff = w["w_gate"].shape[1]
    tok3 = pl.BlockSpec((None, tm, d), lambda i, j, f: (i, j, 0))
    return pl.pallas_call(
        functools.partial(_ffn_body, axis=2),
        out_shape=jax.ShapeDtypeStruct((b, s, d), F32),
        grid=(b, s // tm, dff // tf),
        in_specs=[tok3, tok3, pl.BlockSpec((None, 1, d), lambda i, j, f: (i, 0, 5)),
                  pl.BlockSpec((d, tf), lambda i, j, f: (0, f)),
                  pl.BlockSpec((d, tf), lambda i, j, f: (0, f)),
                  pl.BlockSpec((tf, d), lambda i, j, f: (f, 0))],
        out_specs=tok3,
        compiler_params=_params(("arbitrary", "arbitrary", "arbitrary")),
        name="ffn_prompt",
    )(h2, x1, mod_p, w["w_gate"], w["w_up"], w["w_down"])


def _layer_out_sample(x, attn_o, pool_o, mod_s, w):
    n, d = x.shape
    bd = mod_s.shape[0]
    full = lambda *shape: pl.BlockSpec(shape, lambda *_: (0,) * len(shape))
    mod = lambda col: pl.BlockSpec((bd, d), lambda *_: (0, col))
    x1, h2 = pl.pallas_call(
        _mix_kernel,
        out_shape=(jax.ShapeDtypeStruct((n, d), F32), jax.ShapeDtypeStruct((n, d), BF16)),
        grid=(1,),
        in_specs=[full(n, d), full(*attn_o.shape), full(*pool_o.shape), _const_spec(w["w_out"]),
                  mod(2), mod(3), mod(4), _const_spec(w["norm2"])],
        out_specs=(full(n, d), full(n, d)),
        compiler_params=_params(("arbitrary",)),
        name="mix_sample",
    )(x, attn_o, pool_o, w["w_out"], mod_s, mod_s, mod_s, w["norm2"])

    tf = FFN_HIDDEN_TILE
    dff = w["w_gate"].shape[1]
    return pl.pallas_call(
        functools.partial(_ffn_body, axis=0),
        out_shape=jax.ShapeDtypeStruct((n, d), F32),
        grid=(dff // tf,),
        in_specs=[full(n, d), full(n, d), mod(5),
                  pl.BlockSpec((d, tf), lambda f: (0, f)),
                  pl.BlockSpec((d, tf), lambda f: (0, f)),
                  pl.BlockSpec((tf, d), lambda f: (f, 0))],
        out_specs=full(n, d),
        compiler_params=_params(("arbitrary",)),
        name="ffn_sample",
    )(h2, x1, mod_s, w["w_gate"], w["w_up"], w["w_down"])


def _rope_tables(pos):
    inv = 1.0 / (ROPE_THETA ** (jnp.arange(0, QK_ROPE, 2, dtype=F32) / QK_ROPE))
    ang = pos.astype(F32)[:, None] * inv[None, :]
    c, s = jnp.cos(ang), jnp.sin(ang)
    z = jnp.zeros((pos.shape[0], LANES - QK_ROPE), F32)
    return jnp.concatenate([c, c, z], axis=1), jnp.concatenate([-s, s, z], axis=1)


def _layer_weights(l, w_in, q_a_norm, w_q_b, kv_a_norm, w_kv_b, q_norm_nope, q_norm_rope, k_norm_nope, k_norm_rope,
                   w_pool, b_pool, pool_scale, w_out, norm1, norm2, w_gate, w_up, w_down):
    half = QK_ROPE // 2
    o_kr = Q_LORA + KV_LORA
    o_u = o_kr + QK_ROPE
    win = w_in[l]
    w_in_p = jnp.concatenate([win[:, :o_u], win[:, o_kr + half:o_u], win[:, o_kr:o_kr + half], win[:, o_u:]], axis=1)
    wq = w_q_b[l].reshape(Q_LORA, N_HEADS, QK_HEAD)
    w_q_p = jnp.concatenate([wq, wq[:, :, QK_NOPE + half:], wq[:, :, QK_NOPE:QK_NOPE + half]], axis=2)
    wkv = w_kv_b[l].reshape(KV_LORA, N_HEADS, QK_NOPE + V_HEAD)
    zpad = jnp.zeros((HEAD_PAD - QK_HEAD,), F32)
    row = lambda a: a.reshape(1, -1)
    return {
        "norm1": row(norm1[l]), "norm2": row(norm2[l]),
        "w_in": w_in_p.astype(BF16),
        "q_a_norm": row(q_a_norm[l]), "kv_a_norm": row(kv_a_norm[l]),
        "w_q": w_q_p.reshape(Q_LORA, N_HEADS * HEAD_PAD).astype(BF16),
        "w_kv": w_kv_b[l].astype(BF16),
        "w_kt": wkv[:, :, :QK_NOPE].reshape(KV_LORA, N_HEADS * QK_NOPE).T.astype(BF16),
        "w_v": wkv[:, :, QK_NOPE:].transpose(1, 0, 2).astype(BF16),
        "q_gain": row(jnp.concatenate([q_norm_nope[l], q_norm_rope[l], q_norm_rope[l], zpad]) * SCALE),
        "k_gain": row(jnp.concatenate([k_norm_nope[l], k_norm_rope[l], k_norm_rope[l], zpad])),
        "w_pool": w_pool[l].astype(BF16), "b_pool": row(b_pool[l]), "pool_scale": row(pool_scale[l]),
        "w_out": w_out[l].astype(BF16),
        "w_gate": w_gate[l].astype(BF16), "w_up": w_up[l].astype(BF16), "w_down": w_down[l].astype(BF16),
    }


def kernel(x_prompt, x_sample, cache_kv_latent, cache_k_rope, state_pool, page_table, c_prompt, c_sample,
           w_mod, b_mod, norm1, w_in, q_a_norm, w_q_b, kv_a_norm, w_kv_b,
           q_norm_nope, q_norm_rope, k_norm_nope, k_norm_rope,
           w_pool, b_pool, pool_scale, w_out, norm2, w_gate, w_up, w_down):
    b, s, d = x_prompt.shape
    bd, t, _ = x_sample.shape
    depth = w_mod.shape[0]
    past = page_table.shape[1] * cache_kv_latent.shape[2]
    cos_p, sin_p = _rope_tables(jnp.arange(s))
    cos_s, sin_s = _rope_tables(jnp.repeat(past + jnp.arange(t), bd))
    c_all = jnp.concatenate([c_prompt, c_sample], axis=0)
    cache_krt = jnp.swapaxes(cache_k_rope, 2, 3)

    yp = x_prompt
    ys = x_sample.transpose(1, 0, 2).reshape(t * bd, d)
    outs = [[] for _ in range(6)]
    for l in range(depth):
        w = _layer_weights(l, w_in, q_a_norm, w_q_b, kv_a_norm, w_kv_b, q_norm_nope, q_norm_rope, k_norm_nope,
                           k_norm_rope, w_pool, b_pool, pool_scale, w_out, norm1, norm2, w_gate, w_up, w_down)
        mod = _modulation(c_all, w_mod[l], b_mod[l].reshape(1, -1))
        mod_p = mod[:b].reshape(b, 1, -1)
        mod_s = mod[b:]

        q, k, v, lat_p, kr_p, pool_p, utail = _pre_prompt(yp, mod_p, cos_p, sin_p, w)
        attn_p = _attention_prompt(q, k, v)
        yp = _layer_out_prompt(yp, attn_p, pool_p, mod_p, w)

        qabs, qr, lat_s, kr_s, u_s = _pre_sample(ys, mod_s, cos_s, sin_s, w)
        per_seq = lambda a: a.reshape(N_HEADS, t, bd, -1).transpose(2, 1, 0, 3).reshape(bd, t * N_HEADS, -1)
        lat_s = lat_s.reshape(t, bd, -1).transpose(1, 0, 2)
        kr_s = kr_s.reshape(t, bd, -1).transpose(1, 0, 2)
        lat_new = jnp.pad(lat_s, ((0, 0), (0, SUBLANES - t), (0, 0)))
        krt_new = jnp.pad(kr_s.transpose(0, 2, 1), ((0, 0), (0, 0), (0, LANES - t)))
        ctx = _attention_sample(page_table, per_seq(qabs), per_seq(qr), lat_new, krt_new, w["w_kt"],
                                cache_kv_latent, cache_krt, l, t)
        attn_s = _ctx_to_heads(ctx.reshape(bd, t, N_HEADS, -1).transpose(1, 2, 0, 3), w["w_v"])
        u_s = u_s.reshape(t, bd, -1)
        pool_s = _pool_sample(u_s, state_pool[l].transpose(1, 0, 2), w, past)
        ys = _layer_out_sample(ys, attn_s, pool_s, mod_s, w)

        full_s = jnp.concatenate([state_pool[l], u_s.transpose(1, 0, 2)], axis=1)
        for dst, val in zip(outs, (lat_p, kr_p, utail[:, HIST_PAD - POOL_HIST:], lat_s, kr_s,
                                   full_s[:, -POOL_HIST:])):
            dst.append(val)

    ys = ys.reshape(t, bd, d).transpose(1, 0, 2)
    return (yp, ys) + tuple(jnp.stack(o) for o in outs)
```

```python
import functools

import jax
import jax.numpy as jnp
from jax import lax
from jax.experimental import pallas as pl
from jax.experimental.pallas import tpu as pltpu

F32 = jnp.float32
BF16 = jnp.bfloat16

N_HEADS = 8
QK_NOPE = 128
QK_ROPE = 64
QK_HEAD = QK_NOPE + QK_ROPE
V_HEAD = 128
Q_LORA = 512
KV_LORA = 256
POOL_WINDOWS = (2, 4, 8, 16)
POOL_HIST = max(POOL_WINDOWS) - 1
ROPE_THETA = 10000.0
EPS = 1e-6
SCALE = QK_HEAD ** -0.5

LANES = 128
SUBLANES = 8
HEAD_PAD = 2 * LANES
assert all(w & (w - 1) == 0 for w in POOL_WINDOWS)
HIST_PAD = SUBLANES * (max(POOL_WINDOWS).bit_length() - 1)
VMEM_LIMIT_BYTES = 56 * 1024 * 1024

PROMPT_TILE = 512
ATTN_TILE = 512
FFN_TOKEN_TILE = 512
FFN_HIDDEN_TILE = 512
CHUNK_PAGES = 16
PREFETCH_CHUNKS = 3
CACHE_SLOTS = PREFETCH_CHUNKS + 3
SEQ_STREAMS = 2

_NT = (((1,), (1,)), ((), ()))


def _rms(x, w):
    return x * lax.rsqrt(jnp.mean(x * x, axis=-1, keepdims=True) + EPS) * w


def _rows(a, n):
    r = a.shape[0]
    if r == 1 or r == n:
        return a
    return jnp.tile(a, (n // r, 1))


def _rope(blk, cosx, sinx):
    return blk * cosx + pltpu.roll(blk, LANES // 2, axis=1) * sinx


def _params(semantics):
    return pltpu.CompilerParams(dimension_semantics=semantics, vmem_limit_bytes=VMEM_LIMIT_BYTES)


def _mod_kernel(c_ref, w_ref, b_ref, o_ref):
    c = c_ref[...]
    a = (c * jax.nn.sigmoid(c)).astype(BF16)
    o_ref[...] = jnp.dot(a, w_ref[...].astype(BF16), preferred_element_type=F32) + b_ref[...]


def _modulation(c_all, w_mod, b_mod):
    n, d = c_all.shape
    width = w_mod.shape[1]
    tn = 1024
    return pl.pallas_call(
        _mod_kernel,
        out_shape=jax.ShapeDtypeStruct((n, width), F32),
        grid=(width // tn,),
        in_specs=[
            pl.BlockSpec((n, d), lambda j: (0, 0)),
            pl.BlockSpec((d, tn), lambda j: (0, j)),
            pl.BlockSpec((1, tn), lambda j: (0, j)),
        ],
        out_specs=pl.BlockSpec((n, tn), lambda j: (0, j)),
        compiler_params=_params(("arbitrary",)),
        name="modulation",
    )(c_all, w_mod, b_mod)


def _window_sum(ubuf, stages, c0, pg, w, tm):
    n_stage = w.bit_length() - 1
    read = lambda lo, n: ubuf[lo:lo + n, c0:c0 + pg]
    for k in range(n_stage):
        s = 1 << k
        if k == n_stage - 1:
            return read(HIST_PAD, tm) + read(HIST_PAD - s, tm)
        lo = SUBLANES * (k + 1)
        n = HIST_PAD + tm - lo
        dst = stages[k % 2]
        dst[lo:lo + n, :] = read(lo, n) + read(lo - s, n)
        read = lambda lo, n, src=dst: src[lo:lo + n, :]


def _pool_prompt(u, ubuf, stages, s_idx, tm, wpool_ref, bpool_ref, pscale_ref, pool_o_ref):
    pw = u.shape[1]
    pg = pw // len(POOL_WINDOWS)
    ubuf[HIST_PAD:HIST_PAD + tm, :] = u
    pos = s_idx * tm + lax.broadcasted_iota(jnp.int32, (tm, 1), 0)
    for g, w in enumerate(POOL_WINDOWS):
        c0 = g * pg
        ug = u[:, c0:c0 + pg]
        acc = _window_sum(ubuf, stages, c0, pg, w, tm)
        cnt = jnp.minimum(pos + 1, w).astype(F32)
        d = acc / cnt - ug
        o = jnp.dot(d.astype(BF16), wpool_ref[g], preferred_element_type=F32) + bpool_ref[:, c0:c0 + pg]
        pool_o_ref[:, c0:c0 + pg] = (o * pscale_ref[:, c0:c0 + pg]).astype(BF16)
    ubuf[0:HIST_PAD, :] = ubuf[tm:tm + HIST_PAD, :]


def _pre_kernel(*refs, sample, tm):
    (x_ref, sh1_ref, sc1_ref, norm1_ref, win_ref, qan_ref, wq_ref, kvan_ref, cos_ref, sin_ref, qg_ref, kg_ref,
     *rest) = refs
    if sample:
        wkt_ref, qabs_ref, qr_ref, lat_ref, kr_ref, u_ref = rest
    else:
        (wkv_ref, wpool_ref, bpool_ref, pscale_ref,
         q_ref, k_ref, v_ref, lat_ref, kr_ref, pool_o_ref, utail_ref, ubuf, stage_a, stage_b) = rest
        s_idx = pl.program_id(1)

        @pl.when(s_idx == 0)
        def _():
            ubuf[0:HIST_PAD, :] = jnp.zeros((HIST_PAD, ubuf.shape[1]), F32)

    o_kv = Q_LORA
    o_kr = o_kv + KV_LORA
    o_u = o_kr + LANES
    x = x_ref[...]
    h = _rms(x, norm1_ref[...]) * (1.0 + _rows(sc1_ref[...], tm)) + _rows(sh1_ref[...], tm)
    h = h.astype(BF16)
    cosx = cos_ref[...]
    sinx = sin_ref[...]
    qg = qg_ref[...]
    kg = kg_ref[...]

    cq = jnp.dot(h, win_ref[:, 0:o_kv], preferred_element_type=F32)
    q = jnp.dot(_rms(cq, qan_ref[...]).astype(BF16), wq_ref[...], preferred_element_type=F32)
    for hd in range(N_HEADS):
        c0 = hd * HEAD_PAD
        qn = q[:, c0:c0 + QK_NOPE]
        qr = _rope(q[:, c0 + QK_NOPE:c0 + HEAD_PAD], cosx, sinx)
        r = lax.rsqrt(jnp.sum(qn * qn + qr * qr, axis=-1, keepdims=True) * (1.0 / QK_HEAD) + EPS)
        qn = qn * r * qg[:, 0:QK_NOPE]
        qr = qr * r * qg[:, QK_NOPE:]
        if sample:
            qk = (qn * kg[:, 0:QK_NOPE]).astype(BF16)
            qabs = jnp.dot(qk, wkt_ref[hd * QK_NOPE:(hd + 1) * QK_NOPE, :], preferred_element_type=F32)
            qabs_ref[hd] = qabs.astype(BF16)
            qr_ref[hd] = (qr * kg[:, QK_NOPE:]).astype(BF16)
        else:
            q_ref[hd] = jnp.concatenate([qn, qr], axis=-1).astype(BF16)

    zk = jnp.dot(h, win_ref[:, o_kv:o_u], preferred_element_type=F32)
    lat = _rms(zk[:, 0:KV_LORA], kvan_ref[...])
    lat_ref[...] = lat
    kr = _rope(zk[:, KV_LORA:], cosx, sinx)
    kr_ref[...] = kr[:, 0:QK_ROPE]
    if not sample:
        kv = jnp.dot(lat.astype(BF16), wkv_ref[...], preferred_element_type=F32)
        kr_sq = kr * kr
        for hd in range(N_HEADS):
            c0 = hd * HEAD_PAD
            kn = kv[:, c0:c0 + QK_NOPE]
            rk = lax.rsqrt(jnp.sum(kn * kn + kr_sq, axis=-1, keepdims=True) * (1.0 / QK_HEAD) + EPS)
            k_ref[hd] = jnp.concatenate([kn * rk * kg[:, 0:QK_NOPE], kr * rk * kg[:, QK_NOPE:]], axis=-1).astype(BF16)
            v_ref[hd] = kv[:, c0 + QK_NOPE:c0 + HEAD_PAD].astype(BF16)

    u = jnp.dot(h, win_ref[:, o_u:], preferred_element_type=F32)
    if sample:
        u_ref[...] = u
    else:
        _pool_prompt(u, ubuf, (stage_a, stage_b), s_idx, tm, wpool_ref, bpool_ref, pscale_ref, pool_o_ref)

        @pl.when(s_idx == pl.num_programs(1) - 1)
        def _():
            utail_ref[...] = ubuf[0:HIST_PAD, :]


def _const_spec(a):
    nd = a.ndim
    return pl.BlockSpec(a.shape, lambda *_: (0,) * nd)


def _pre_prompt(x, mod_p, cosx, sinx, w):
    b, s, d = x.shape
    tm = PROMPT_TILE
    pw = w["pool_scale"].shape[1]
    consts_a = [w["norm1"], w["w_in"], w["q_a_norm"], w["w_q"], w["kv_a_norm"]]
    consts_b = [w["q_gain"], w["k_gain"], w["w_kv"], w["w_pool"], w["b_pool"], w["pool_scale"]]
    tok = lambda width: pl.BlockSpec((None, tm, width), lambda i, j: (i, j, 0))
    head = lambda width: pl.BlockSpec((None, N_HEADS, tm, width), lambda i, j: (i, 0, j, 0))
    mod = lambda col: pl.BlockSpec((None, 1, d), lambda i, j: (i, 0, col))
    rope = pl.BlockSpec((tm, LANES), lambda i, j: (j, 0))
    return pl.pallas_call(
        functools.partial(_pre_kernel, sample=False, tm=tm),
        out_shape=(
            jax.ShapeDtypeStruct((b, N_HEADS, s, HEAD_PAD), BF16),
            jax.ShapeDtypeStruct((b, N_HEADS, s, HEAD_PAD), BF16),
            jax.ShapeDtypeStruct((b, N_HEADS, s, V_HEAD), BF16),
            jax.ShapeDtypeStruct((b, s, KV_LORA), F32),
            jax.ShapeDtypeStruct((b, s, QK_ROPE), F32),
            jax.ShapeDtypeStruct((b, s, pw), BF16),
            jax.ShapeDtypeStruct((b, HIST_PAD, pw), F32),
        ),
        grid=(b, s // tm),
        in_specs=[tok(d), mod(0), mod(1)] + [_const_spec(a) for a in consts_a] + [rope, rope]
        + [_const_spec(a) for a in consts_b],
        out_specs=(head(HEAD_PAD), head(HEAD_PAD), head(V_HEAD), tok(KV_LORA), tok(QK_ROPE), tok(pw),
                   pl.BlockSpec((None, HIST_PAD, pw), lambda i, j: (i, 0, 0))),
        scratch_shapes=[pltpu.VMEM((tm + HIST_PAD, pw), F32)]
        + [pltpu.VMEM((tm + HIST_PAD, pw // len(POOL_WINDOWS)), F32)] * 2,
        compiler_params=_params(("arbitrary", "arbitrary")),
        name="pre_prompt",
    )(x, mod_p, mod_p, *consts_a, cosx, sinx, *consts_b)


def _pre_sample(x, mod_s, cosx, sinx, w):
    n, d = x.shape
    bd = mod_s.shape[0]
    pw = w["pool_scale"].shape[1]
    consts = [w["norm1"], w["w_in"], w["q_a_norm"], w["w_q"], w["kv_a_norm"]]
    mod = lambda col: pl.BlockSpec((bd, d), lambda i: (0, col))
    full = lambda *shape: pl.BlockSpec(shape, lambda i: (0,) * len(shape))
    return pl.pallas_call(
        functools.partial(_pre_kernel, sample=True, tm=n),
        out_shape=(
            jax.ShapeDtypeStruct((N_HEADS, n, KV_LORA), BF16),
            jax.ShapeDtypeStruct((N_HEADS, n, LANES), BF16),
            jax.ShapeDtypeStruct((n, KV_LORA), F32),
            jax.ShapeDtypeStruct((n, QK_ROPE), F32),
            jax.ShapeDtypeStruct((n, pw), F32),
        ),
        grid=(1,),
        in_specs=[full(n, d), mod(0), mod(1)] + [_const_spec(a) for a in consts]
        + [full(n, LANES), full(n, LANES), _const_spec(w["q_gain"]), _const_spec(w["k_gain"]), _const_spec(w["w_kt"])],
        out_specs=(full(N_HEADS, n, KV_LORA), full(N_HEADS, n, LANES), full(n, KV_LORA), full(n, QK_ROPE),
                   full(n, pw)),
        compiler_params=_params(("arbitrary",)),
        name="pre_sample",
    )(x, mod_s, mod_s, *consts, cosx, sinx, w["q_gain"], w["k_gain"], w["w_kt"])


def _attn_kernel(q_ref, k_ref, v_ref, o_ref, *, seq, tile):
    n = seq // tile
    row = lax.broadcasted_iota(jnp.int32, (tile, tile), 0)
    col = lax.broadcasted_iota(jnp.int32, (tile, tile), 1)
    for qi in range(n):
        q = q_ref[qi * tile:(qi + 1) * tile, :]
        m = jnp.full((tile, 1), -jnp.inf, F32)
        l = jnp.zeros((tile, 1), F32)
        acc = jnp.zeros((tile, V_HEAD), F32)
        for ki in range(qi + 1):
            k = k_ref[ki * tile:(ki + 1) * tile, :]
            s = lax.dot_general(q, k, _NT, preferred_element_type=F32)
            if ki == qi:
                s = jnp.where(col <= row, s, -jnp.inf)
            m_new = jnp.maximum(m, jnp.max(s, axis=-1, keepdims=True))
            alpha = jnp.exp(m - m_new)
            p = jnp.exp(s - m_new)
            l = alpha * l + jnp.sum(p, axis=-1, keepdims=True)
            acc = alpha * acc + jnp.dot(p.astype(BF16), v_ref[ki * tile:(ki + 1) * tile, :],
                                        preferred_element_type=F32)
            m = m_new
        o_ref[qi * tile:(qi + 1) * tile, :] = (acc / l).astype(BF16)


def _attention_prompt(q, k, v):
    b, h, s, _ = q.shape
    head = lambda width: pl.BlockSpec((None, None, s, width), lambda i, j: (i, j, 0, 0))
    return pl.pallas_call(
        functools.partial(_attn_kernel, seq=s, tile=min(ATTN_TILE, s)),
        out_shape=jax.ShapeDtypeStruct((b, s, h * V_HEAD), BF16),
        grid=(b, h),
        in_specs=[head(HEAD_PAD), head(HEAD_PAD), head(V_HEAD)],
        out_specs=pl.BlockSpec((None, s, V_HEAD), lambda i, j: (i, 0, j)),
        compiler_params=_params(("arbitrary", "arbitrary")),
        name="attention_prompt",
    )(q, k, v)


def _sattn_kernel(pt_ref, q_ref, qr_ref, latn_ref, krn_ref, wkt_ref, clat_hbm, ckr_hbm, ctx_ref,
                  lat_buf, kr_buf, sem, lhs, s_even, s_odd, *, layer, n_seq, n_pages, page, n_new):
    b = pl.program_id(0)
    n_chunks = n_pages // CHUNK_PAGES
    per_stream = (n_seq // SEQ_STREAMS) * n_chunks
    g0 = b * n_chunks
    streams = range(SEQ_STREAMS)
    nq = q_ref.shape[1]
    n_k = wkt_ref.shape[0]
    s_bufs = (s_even, s_odd)

    def slot_of(g):
        return lax.rem(g, CACHE_SLOTS)

    def copies(st, g):
        slot = slot_of(g)
        out = []
        for p in range(CHUNK_PAGES):
            phys = pt_ref[(st * per_stream + g) * CHUNK_PAGES + p]
            dst = pl.ds(p * page, page)
            out.append(pltpu.make_async_copy(clat_hbm.at[layer, phys], lat_buf.at[st, slot, dst, :],
                                             sem.at[st, 0, slot]))
            out.append(pltpu.make_async_copy(ckr_hbm.at[layer, phys], kr_buf.at[st, slot, :, dst],
                                             sem.at[st, 1, slot]))
        return out

    def start(st, g):
        for cp in copies(st, g):
            cp.start()

    def wait(st, g):
        for cp in copies(st, g):
            cp.wait()

    def fetch_ahead(st, g):
        @pl.when(g + PREFETCH_CHUNKS < per_stream)
        def _():
            start(st, g + PREFETCH_CHUNKS)

    @pl.when(b == 0)
    def _():
        for st in streams:
            lhs[st, 0:n_k, :] = wkt_ref[...]
            for g in range(min(PREFETCH_CHUNKS, per_stream)):
                start(st, g)

    for st in streams:
        lhs[st, n_k:n_k + nq, :] = q_ref[st]

    def scores(st, lat, krt):
        t = lat.shape[0]
        big = lax.dot_general(lhs[st], lat.astype(BF16), _NT, preferred_element_type=F32)
        kn = big[0:n_k]
        ssq = jnp.sum((kn * kn).reshape(N_HEADS, QK_NOPE, t), axis=1)
        ssq = ssq + jnp.sum(krt * krt, axis=0, keepdims=True)
        r = lax.rsqrt(ssq * (1.0 / QK_HEAD) + EPS)
        sr = jnp.dot(qr_ref[st, :, 0:QK_ROPE], krt.astype(BF16), preferred_element_type=F32)
        return (big[n_k:] + sr) * jnp.tile(r, (nq // N_HEADS, 1))

    def update(s, lat, carry):
        m, l, acc = carry
        m_new = jnp.maximum(m, jnp.max(s, axis=-1, keepdims=True))
        alpha = jnp.exp(m - m_new)
        p = jnp.exp(s - m_new)
        l = alpha * l + jnp.sum(p, axis=-1, keepdims=True)
        acc = alpha * acc + jnp.dot(p.astype(BF16), lat.astype(BF16), preferred_element_type=F32)
        return m_new, l, acc

    for st in streams:
        wait(st, g0)
        fetch_ahead(st, g0)

    pad = LANES - latn_ref.shape[1]
    key = lax.broadcasted_iota(jnp.int32, (nq, LANES), 1)
    tok = lax.broadcasted_iota(jnp.int32, (nq, LANES), 0) // N_HEADS
    carries = []
    for st in streams:
        lat_new = jnp.concatenate([latn_ref[st], jnp.zeros((pad, KV_LORA), F32)], axis=0)
        s_new = jnp.where((key <= tok) & (key < n_new), scores(st, lat_new, krn_ref[st]), -jnp.inf)
        carry = (jnp.full((nq, 1), -jnp.inf, F32), jnp.zeros((nq, 1), F32), jnp.zeros((nq, KV_LORA), F32))
        carries.append(update(s_new, lat_new, carry))

    for st in streams:
        s_even[st] = scores(st, lat_buf[st, slot_of(g0)], kr_buf[st, slot_of(g0)])

    def enter(c):
        for st in streams:
            wait(st, g0 + c)
            fetch_ahead(st, g0 + c)

    def step(c, parity, carries):
        g = g0 + c
        out = []
        for st in streams:
            s_bufs[parity][st] = scores(st, lat_buf[st, slot_of(g)], kr_buf[st, slot_of(g)])
            out.append(update(s_bufs[1 - parity][st], lat_buf[st, slot_of(g - 1)], carries[st]))
        return tuple(out)

    def two_steps(i, carries):
        c = 1 + 2 * i
        enter(c)
        enter(c + 1)
        return step(c + 1, 0, step(c, 1, carries))

    carries = lax.fori_loop(0, (n_chunks - 1) // 2, two_steps, tuple(carries))
    if n_chunks % 2 == 0:
        enter(n_chunks - 1)
        carries = step(n_chunks - 1, 1, carries)
    for st in streams:
        m, l, acc = update(s_bufs[(n_chunks - 1) % 2][st], lat_buf[st, slot_of(g0 + n_chunks - 1)], carries[st])
        ctx_ref[st] = acc / l


def _attention_sample(page_table, qabs, qr, lat_new, krt_new, w_kt, cache_lat, cache_krt, layer, n_new):
    bd, nq, _ = qabs.shape
    n_pages = page_table.shape[1]
    page = cache_lat.shape[2]
    assert n_pages % CHUNK_PAGES == 0 and n_pages // CHUNK_PAGES >= 2 and bd % SEQ_STREAMS == 0
    rows = CHUNK_PAGES * page
    n_k = w_kt.shape[0]
    pt = page_table.reshape(bd // SEQ_STREAMS, SEQ_STREAMS, n_pages).transpose(1, 0, 2).reshape(-1)
    per_step = lambda r, c: pl.BlockSpec((SEQ_STREAMS, r, c), lambda i, pt: (i, 0, 0))
    grid_spec = pltpu.PrefetchScalarGridSpec(
        num_scalar_prefetch=1,
        grid=(bd // SEQ_STREAMS,),
        in_specs=[
            per_step(nq, KV_LORA), per_step(nq, LANES), per_step(SUBLANES, KV_LORA), per_step(QK_ROPE, LANES),
            pl.BlockSpec((n_k, KV_LORA), lambda i, pt: (0, 0)),
            pl.BlockSpec(memory_space=pl.ANY), pl.BlockSpec(memory_space=pl.ANY),
        ],
        out_specs=per_step(nq, KV_LORA),
        scratch_shapes=[
            pltpu.VMEM((SEQ_STREAMS, CACHE_SLOTS, rows, KV_LORA), F32),
            pltpu.VMEM((SEQ_STREAMS, CACHE_SLOTS, QK_ROPE, rows), F32),
            pltpu.SemaphoreType.DMA((SEQ_STREAMS, 2, CACHE_SLOTS)),
            pltpu.VMEM((SEQ_STREAMS, n_k + nq, KV_LORA), BF16),
            pltpu.VMEM((SEQ_STREAMS, nq, rows), F32),
            pltpu.VMEM((SEQ_STREAMS, nq, rows), F32),
        ],
    )
    return pl.pallas_call(
        functools.partial(_sattn_kernel, layer=layer, n_seq=bd, n_pages=n_pages, page=page, n_new=n_new),
        out_shape=jax.ShapeDtypeStruct((bd, nq, KV_LORA), F32),
        grid_spec=grid_spec,
        compiler_params=_params(("arbitrary",)),
        name="attention_sample",
    )(pt, qabs, qr, lat_new, krt_new, w_kt, cache_lat, cache_krt)


def _ctx_to_heads_kernel(ctx_ref, wv_ref, o_ref):
    n_tok, n_head, bd, _ = ctx_ref.shape
    for t in range(n_tok):
        for hd in range(n_head):
            o = jnp.dot(ctx_ref[t, hd].astype(BF16), wv_ref[hd], preferred_element_type=F32)
            o_ref[t * bd:(t + 1) * bd, hd * V_HEAD:(hd + 1) * V_HEAD] = o.astype(BF16)


def _ctx_to_heads(ctx, w_v):
    n_tok, n_head, bd, _ = ctx.shape
    return pl.pallas_call(
        _ctx_to_heads_kernel,
        out_shape=jax.ShapeDtypeStruct((n_tok * bd, n_head * V_HEAD), BF16),
        grid=(1,),
        in_specs=[_const_spec(ctx), _const_spec(w_v)],
        out_specs=pl.BlockSpec((n_tok * bd, n_head * V_HEAD), lambda i: (0, 0)),
        compiler_params=_params(("arbitrary",)),
        name="ctx_to_heads",
    )(ctx, w_v)


def _pool_sample_kernel(u_ref, hist_ref, wpool_ref, bpool_ref, pscale_ref, o_ref, *, first_pos):
    n_tok, bd, pw = u_ref.shape
    n_hist = hist_ref.shape[0]
    pg = pw // len(POOL_WINDOWS)

    def row(i, c0):
        return hist_ref[i, :, c0:c0 + pg] if i < n_hist else u_ref[i - n_hist, :, c0:c0 + pg]

    for t in range(n_tok):
        for g, w in enumerate(POOL_WINDOWS):
            c0 = g * pg
            ug = u_ref[t, :, c0:c0 + pg]
            acc = ug
            for j in range(1, w):
                acc = acc + row(n_hist + t - j, c0)
            d = acc / float(min(first_pos + t + 1, w)) - ug
            o = jnp.dot(d.astype(BF16), wpool_ref[g], preferred_element_type=F32) + bpool_ref[:, c0:c0 + pg]
            o_ref[t * bd:(t + 1) * bd, c0:c0 + pg] = (o * pscale_ref[:, c0:c0 + pg]).astype(BF16)


def _pool_sample(u, hist, w, first_pos):
    n_tok, bd, pw = u.shape
    args = (u, hist, w["w_pool"], w["b_pool"], w["pool_scale"])
    return pl.pallas_call(
        functools.partial(_pool_sample_kernel, first_pos=first_pos),
        out_shape=jax.ShapeDtypeStruct((n_tok * bd, pw), BF16),
        grid=(1,),
        in_specs=[_const_spec(a) for a in args],
        out_specs=pl.BlockSpec((n_tok * bd, pw), lambda i: (0, 0)),
        compiler_params=_params(("arbitrary",)),
        name="pool_sample",
    )(*args)


def _mix_kernel(x_ref, a_ref, p_ref, wout_ref, g1_ref, sh2_ref, sc2_ref, norm2_ref, x1_ref, h2_ref):
    tm = x_ref.shape[0]
    mixed = jnp.concatenate([a_ref[...], p_ref[...]], axis=1)
    mix = jnp.dot(mixed, wout_ref[...], preferred_element_type=F32)
    x1 = x_ref[...] + _rows(g1_ref[...], tm) * mix
    x1_ref[...] = x1
    h2 = _rms(x1, norm2_ref[...]) * (1.0 + _rows(sc2_ref[...], tm)) + _rows(sh2_ref[...], tm)
    h2_ref[...] = h2.astype(BF16)


def _ffn_body(h_ref, x1_ref, g2_ref, wg_ref, wu_ref, wd_ref, y_ref, *, axis):
    @pl.when(pl.program_id(axis) == 0)
    def _():
        y_ref[...] = x1_ref[...]

    h = h_ref[...]
    gate = jnp.dot(h, wg_ref[...], preferred_element_type=F32)
    up = jnp.dot(h, wu_ref[...], preferred_element_type=F32)
    act = (gate * jax.nn.sigmoid(gate) * up).astype(BF16)
    part = jnp.dot(act, wd_ref[...], preferred_element_type=F32)
    y_ref[...] += _rows(g2_ref[...], y_ref.shape[0]) * part


def _layer_out_prompt(x, attn_o, pool_o, mod_p, w):
    b, s, d = x.shape
    tm = PROMPT_TILE
    tok = lambda width: pl.BlockSpec((None, tm, width), lambda i, j: (i, j, 0))
    mod = lambda col: pl.BlockSpec((None, 1, d), lambda i, j: (i, 0, col))
    x1, h2 = pl.pallas_call(
        _mix_kernel,
        out_shape=(jax.ShapeDtypeStruct((b, s, d), F32), jax.ShapeDtypeStruct((b, s, d), BF16)),
        grid=(b, s // tm),
        in_specs=[tok(d), tok(attn_o.shape[2]), tok(pool_o.shape[2]), _const_spec(w["w_out"]),
                  mod(2), mod(3), mod(4), _const_spec(w["norm2"])],
        out_specs=(tok(d), tok(d)),
        compiler_params=_params(("arbitrary", "arbitrary")),
        name="mix_prompt",
    )(x, attn_o, pool_o, w["w_out"], mod_p, mod_p, mod_p, w["norm2"])

    tm, tf = FFN_TOKEN_TILE, FFN_HIDDEN_TILE
    dff = w["w_gate"].shape[1]
    tok3 = pl.BlockSpec((None, tm, d), lambda i, j, f: (i, j, 0))
    return pl.pallas_call(
        functools.partial(_ffn_body, axis=2),
        out_shape=jax.ShapeDtypeStruct((b, s, d), F32),
        grid=(b, s // tm, dff // tf),
        in_specs=[tok3, tok3, pl.BlockSpec((None, 1, d), lambda i, j, f: (i, 0, 5)),
                  pl.BlockSpec((d, tf), lambda i, j, f: (0, f)),
                  pl.BlockSpec((d, tf), lambda i, j, f: (0, f)),
                  pl.BlockSpec((tf, d), lambda i, j, f: (f, 0))],
        out_specs=tok3,
        compiler_params=_params(("arbitrary", "arbitrary", "arbitrary")),
        name="ffn_prompt",
    )(h2, x1, mod_p, w["w_gate"], w["w_up"], w["w_down"])


def _layer_out_sample(x, attn_o, pool_o, mod_s, w):
    n, d = x.shape
    bd = mod_s.shape[0]
    full = lambda *shape: pl.BlockSpec(shape, lambda *_: (0,) * len(shape))
    mod = lambda col: pl.BlockSpec((bd, d), lambda *_: (0, col))
    x1, h2 = pl.pallas_call(
        _mix_kernel,
        out_shape=(jax.ShapeDtypeStruct((n, d), F32), jax.ShapeDtypeStruct((n, d), BF16)),
        grid=(1,),
        in_specs=[full(n, d), full(*attn_o.shape), full(*pool_o.shape), _const_spec(w["w_out"]),
                  mod(2), mod(3), mod(4), _const_spec(w["norm2"])],
        out_specs=(full(n, d), full(n, d)),
        compiler_params=_params(("arbitrary",)),
        name="mix_sample",
    )(x, attn_o, pool_o, w["w_out"], mod_s, mod_s, mod_s, w["norm2"])

    tf = FFN_HIDDEN_TILE
    dff = w["w_gate"].shape[1]
    return pl.pallas_call(
        functools.partial(_ffn_body, axis=0),
        out_shape=jax.ShapeDtypeStruct((n, d), F32),
        grid=(dff // tf,),
        in_specs=[full(n, d), full(n, d), mod(5),
                  pl.BlockSpec((d, tf), lambda f: (0, f)),
                  pl.BlockSpec((d, tf), lambda f: (0, f)),
                  pl.BlockSpec((tf, d), lambda f: (f, 0))],
        out_specs=full(n, d),
        compiler_params=_params(("arbitrary",)),
        name="ffn_sample",
    )(h2, x1, mod_s, w["w_gate"], w["w_up"], w["w_down"])


def _rope_tables(pos):
    inv = 1.0 / (ROPE_THETA ** (jnp.arange(0, QK_ROPE, 2, dtype=F32) / QK_ROPE))
    ang = pos.astype(F32)[:, None] * inv[None, :]
    c, s = jnp.cos(ang), jnp.sin(ang)
    z = jnp.zeros((pos.shape[0], LANES - QK_ROPE), F32)
    return jnp.concatenate([c, c, z], axis=1), jnp.concatenate([-s, s, z], axis=1)


def _layer_weights(l, w_in, q_a_norm, w_q_b, kv_a_norm, w_kv_b, q_norm_nope, q_norm_rope, k_norm_nope, k_norm_rope,
                   w_pool, b_pool, pool_scale, w_out, norm1, norm2, w_gate, w_up, w_down):
    half = QK_ROPE // 2
    o_kr = Q_LORA + KV_LORA
    o_u = o_kr + QK_ROPE
    win = w_in[l]
    w_in_p = jnp.concatenate([win[:, :o_u], win[:, o_kr + half:o_u], win[:, o_kr:o_kr + half], win[:, o_u:]], axis=1)
    wq = w_q_b[l].reshape(Q_LORA, N_HEADS, QK_HEAD)
    w_q_p = jnp.concatenate([wq, wq[:, :, QK_NOPE + half:], wq[:, :, QK_NOPE:QK_NOPE + half]], axis=2)
    wkv = w_kv_b[l].reshape(KV_LORA, N_HEADS, QK_NOPE + V_HEAD)
    zpad = jnp.zeros((HEAD_PAD - QK_HEAD,), F32)
    row = lambda a: a.reshape(1, -1)
    return {
        "norm1": row(norm1[l]), "norm2": row(norm2[l]),
        "w_in": w_in_p.astype(BF16),
        "q_a_norm": row(q_a_norm[l]), "kv_a_norm": row(kv_a_norm[l]),
        "w_q": w_q_p.reshape(Q_LORA, N_HEADS * HEAD_PAD).astype(BF16),
        "w_kv": w_kv_b[l].astype(BF16),
        "w_kt": wkv[:, :, :QK_NOPE].reshape(KV_LORA, N_HEADS * QK_NOPE).T.astype(BF16),
        "w_v": wkv[:, :, QK_NOPE:].transpose(1, 0, 2).astype(BF16),
        "q_gain": row(jnp.concatenate([q_norm_nope[l], q_norm_rope[l], q_norm_rope[l], zpad]) * SCALE),
        "k_gain": row(jnp.concatenate([k_norm_nope[l], k_norm_rope[l], k_norm_rope[l], zpad])),
        "w_pool": w_pool[l].astype(BF16), "b_pool": row(b_pool[l]), "pool_scale": row(pool_scale[l]),
        "w_out": w_out[l].astype(BF16),
        "w_gate": w_gate[l].astype(BF16), "w_up": w_up[l].astype(BF16), "w_down": w_down[l].astype(BF16),
    }


def kernel(x_prompt, x_sample, cache_kv_latent, cache_k_rope, state_pool, page_table, c_prompt, c_sample,
           w_mod, b_mod, norm1, w_in, q_a_norm, w_q_b, kv_a_norm, w_kv_b,
           q_norm_nope, q_norm_rope, k_norm_nope, k_norm_rope,
           w_pool, b_pool, pool_scale, w_out, norm2, w_gate, w_up, w_down):
    b, s, d = x_prompt.shape
    bd, t, _ = x_sample.shape
    depth = w_mod.shape[0]
    past = page_table.shape[1] * cache_kv_latent.shape[2]
    cos_p, sin_p = _rope_tables(jnp.arange(s))
    cos_s, sin_s = _rope_tables(jnp.repeat(past + jnp.arange(t), bd))
    c_all = jnp.concatenate([c_prompt, c_sample], axis=0)
    cache_krt = jnp.swapaxes(cache_k_rope, 2, 3)

    yp = x_prompt
    ys = x_sample.transpose(1, 0, 2).reshape(t * bd, d)
    outs = [[] for _ in range(6)]
    for l in range(depth):
        w = _layer_weights(l, w_in, q_a_norm, w_q_b, kv_a_norm, w_kv_b, q_norm_nope, q_norm_rope, k_norm_nope,
                           k_norm_rope, w_pool, b_pool, pool_scale, w_out, norm1, norm2, w_gate, w_up, w_down)
        mod = _modulation(c_all, w_mod[l], b_mod[l].reshape(1, -1))
        mod_p = mod[:b].reshape(b, 1, -1)
        mod_s = mod[b:]

        q, k, v, lat_p, kr_p, pool_p, utail = _pre_prompt(yp, mod_p, cos_p, sin_p, w)
        attn_p = _attention_prompt(q, k, v)
        yp = _layer_out_prompt(yp, attn_p, pool_p, mod_p, w)

        qabs, qr, lat_s, kr_s, u_s = _pre_sample(ys, mod_s, cos_s, sin_s, w)
        per_seq = lambda a: a.reshape(N_HEADS, t, bd, -1).transpose(2, 1, 0, 3).reshape(bd, t * N_HEADS, -1)
        lat_s = lat_s.reshape(t, bd, -1).transpose(1, 0, 2)
        kr_s = kr_s.reshape(t, bd, -1).transpose(1, 0, 2)
        lat_new = jnp.pad(lat_s, ((0, 0), (0, SUBLANES - t), (0, 0)))
        krt_new = jnp.pad(kr_s.transpose(0, 2, 1), ((0, 0), (0, 0), (0, LANES - t)))
        ctx = _attention_sample(page_table, per_seq(qabs), per_seq(qr), lat_new, krt_new, w["w_kt"],
                                cache_kv_latent, cache_krt, l, t)
        attn_s = _ctx_to_heads(ctx.reshape(bd, t, N_HEADS, -1).transpose(1, 2, 0, 3), w["w_v"])
        u_s = u_s.reshape(t, bd, -1)
        pool_s = _pool_sample(u_s, state_pool[l].transpose(1, 0, 2), w, past)
        ys = _layer_out_sample(ys, attn_s, pool_s, mod_s, w)

        full_s = jnp.concatenate([state_pool[l], u_s.transpose(1, 0, 2)], axis=1)
        for dst, val in zip(outs, (lat_p, kr_p, utail[:, HIST_PAD - POOL_HIST:], lat_s, kr_s,
                                   full_s[:, -POOL_HIST:])):
            dst.append(val)

    ys = ys.reshape(t, bd, d).transpose(1, 0, 2)
    return (yp, ys) + tuple(jnp.stack(o) for o in outs)
```

```python
import functools

import jax
import jax.numpy as jnp
from jax import lax
from jax.experimental import pallas as pl
from jax.experimental.pallas import tpu as pltpu

F32 = jnp.float32
BF16 = jnp.bfloat16

N_HEADS = 8
QK_NOPE = 128
QK_ROPE = 64
QK_HEAD = QK_NOPE + QK_ROPE
V_HEAD = 128
Q_LORA = 512
KV_LORA = 256
POOL_WINDOWS = (2, 4, 8, 16)
POOL_HIST = max(POOL_WINDOWS) - 1
ROPE_THETA = 10000.0
EPS = 1e-6
SCALE = QK_HEAD ** -0.5

LANES = 128
SUBLANES = 8
HEAD_PAD = 2 * LANES
assert all(w & (w - 1) == 0 for w in POOL_WINDOWS)
HIST_PAD = SUBLANES * (max(POOL_WINDOWS).bit_length() - 1)
VMEM_LIMIT_BYTES = 56 * 1024 * 1024

PROMPT_TILE = 512
ATTN_TILE = 512
FFN_TOKEN_TILE = 512
FFN_HIDDEN_TILE = 512
FFN_WEIGHT_SLOTS = 3
CHUNK_PAGES = 16
PREFETCH_CHUNKS = 3
CACHE_SLOTS = PREFETCH_CHUNKS + 3
SEQ_STREAMS = 2

_NT = (((1,), (1,)), ((), ()))


def _rms(x, w):
    return x * lax.rsqrt(jnp.mean(x * x, axis=-1, keepdims=True) + EPS) * w


def _rows(a, n):
    r = a.shape[0]
    if r == 1 or r == n:
        return a
    return jnp.tile(a, (n // r, 1))


def _rope(blk, cosx, sinx):
    return blk * cosx + pltpu.roll(blk, LANES // 2, axis=1) * sinx


def _params(semantics):
    return pltpu.CompilerParams(dimension_semantics=semantics, vmem_limit_bytes=VMEM_LIMIT_BYTES)


def _mod_kernel(c_ref, w_ref, b_ref, o_ref):
    c = c_ref[...]
    a = (c * jax.nn.sigmoid(c)).astype(BF16)
    o_ref[...] = jnp.dot(a, w_ref[...].astype(BF16), preferred_element_type=F32) + b_ref[...]


def _modulation(c_all, w_mod, b_mod):
    n, d = c_all.shape
    width = w_mod.shape[1]
    tn = 1024
    return pl.pallas_call(
        _mod_kernel,
        out_shape=jax.ShapeDtypeStruct((n, width), F32),
        grid=(width // tn,),
        in_specs=[
            pl.BlockSpec((n, d), lambda j: (0, 0)),
            pl.BlockSpec((d, tn), lambda j: (0, j)),
            pl.BlockSpec((1, tn), lambda j: (0, j)),
        ],
        out_specs=pl.BlockSpec((n, tn), lambda j: (0, j)),
        compiler_params=_params(("arbitrary",)),
        name="modulation",
    )(c_all, w_mod, b_mod)


def _window_sum(ubuf, stages, c0, pg, w, tm):
    n_stage = w.bit_length() - 1
    read = lambda lo, n: ubuf[lo:lo + n, c0:c0 + pg]
    for k in range(n_stage):
        s = 1 << k
        if k == n_stage - 1:
            return read(HIST_PAD, tm) + read(HIST_PAD - s, tm)
        lo = SUBLANES * (k + 1)
        n = HIST_PAD + tm - lo
        dst = stages[k % 2]
        dst[lo:lo + n, :] = read(lo, n) + read(lo - s, n)
        read = lambda lo, n, src=dst: src[lo:lo + n, :]


def _pool_prompt(u, ubuf, stages, s_idx, tm, wpool_ref, bpool_ref, pscale_ref, pool_o_ref):
    pw = u.shape[1]
    pg = pw // len(POOL_WINDOWS)
    ubuf[HIST_PAD:HIST_PAD + tm, :] = u
    pos = s_idx * tm + lax.broadcasted_iota(jnp.int32, (tm, 1), 0)
    for g, w in enumerate(POOL_WINDOWS):
        c0 = g * pg
        ug = u[:, c0:c0 + pg]
        acc = _window_sum(ubuf, stages, c0, pg, w, tm)
        cnt = jnp.minimum(pos + 1, w).astype(F32)
        d = acc / cnt - ug
        o = jnp.dot(d.astype(BF16), wpool_ref[g], preferred_element_type=F32) + bpool_ref[:, c0:c0 + pg]
        pool_o_ref[:, c0:c0 + pg] = (o * pscale_ref[:, c0:c0 + pg]).astype(BF16)
    ubuf[0:HIST_PAD, :] = ubuf[tm:tm + HIST_PAD, :]


def _pre_kernel(*refs, sample, tm):
    (x_ref, sh1_ref, sc1_ref, norm1_ref, win_ref, qan_ref, wq_ref, kvan_ref, cos_ref, sin_ref, qg_ref, kg_ref,
     *rest) = refs
    if sample:
        wkt_ref, qabs_ref, qr_ref, lat_ref, kr_ref, u_ref = rest
    else:
        (wkv_ref, wpool_ref, bpool_ref, pscale_ref,
         q_ref, k_ref, v_ref, lat_ref, kr_ref, pool_o_ref, utail_ref, ubuf, stage_a, stage_b) = rest
        s_idx = pl.program_id(1)

        @pl.when(s_idx == 0)
        def _():
            ubuf[0:HIST_PAD, :] = jnp.zeros((HIST_PAD, ubuf.shape[1]), F32)

    o_kv = Q_LORA
    o_kr = o_kv + KV_LORA
    o_u = o_kr + LANES
    x = x_ref[...]
    h = _rms(x, norm1_ref[...]) * (1.0 + _rows(sc1_ref[...], tm)) + _rows(sh1_ref[...], tm)
    h = h.astype(BF16)
    cosx = cos_ref[...]
    sinx = sin_ref[...]
    qg = qg_ref[...]
    kg = kg_ref[...]

    cq = jnp.dot(h, win_ref[:, 0:o_kv], preferred_element_type=F32)
    q = jnp.dot(_rms(cq, qan_ref[...]).astype(BF16), wq_ref[...], preferred_element_type=F32)
    for hd in range(N_HEADS):
        c0 = hd * HEAD_PAD
        qn = q[:, c0:c0 + QK_NOPE]
        qr = _rope(q[:, c0 + QK_NOPE:c0 + HEAD_PAD], cosx, sinx)
        r = lax.rsqrt(jnp.sum(qn * qn + qr * qr, axis=-1, keepdims=True) * (1.0 / QK_HEAD) + EPS)
        qn = qn * r * qg[:, 0:QK_NOPE]
        qr = qr * r * qg[:, QK_NOPE:]
        if sample:
            qk = (qn * kg[:, 0:QK_NOPE]).astype(BF16)
            qabs = jnp.dot(qk, wkt_ref[hd * QK_NOPE:(hd + 1) * QK_NOPE, :], preferred_element_type=F32)
            qabs_ref[hd] = qabs.astype(BF16)
            qr_ref[hd] = (qr * kg[:, QK_NOPE:]).astype(BF16)
        else:
            q_ref[hd] = jnp.concatenate([qn, qr], axis=-1).astype(BF16)

    zk = jnp.dot(h, win_ref[:, o_kv:o_u], preferred_element_type=F32)
    lat = _rms(zk[:, 0:KV_LORA], kvan_ref[...])
    lat_ref[...] = lat
    kr = _rope(zk[:, KV_LORA:], cosx, sinx)
    kr_ref[...] = kr[:, 0:QK_ROPE]
    if not sample:
        kv = jnp.dot(lat.astype(BF16), wkv_ref[...], preferred_element_type=F32)
        kr_sq = kr * kr
        for hd in range(N_HEADS):
            c0 = hd * HEAD_PAD
            kn = kv[:, c0:c0 + QK_NOPE]
            rk = lax.rsqrt(jnp.sum(kn * kn + kr_sq, axis=-1, keepdims=True) * (1.0 / QK_HEAD) + EPS)
            k_ref[hd] = jnp.concatenate([kn * rk * kg[:, 0:QK_NOPE], kr * rk * kg[:, QK_NOPE:]], axis=-1).astype(BF16)
            v_ref[hd] = kv[:, c0 + QK_NOPE:c0 + HEAD_PAD].astype(BF16)

    u = jnp.dot(h, win_ref[:, o_u:], preferred_element_type=F32)
    if sample:
        u_ref[...] = u
    else:
        _pool_prompt(u, ubuf, (stage_a, stage_b), s_idx, tm, wpool_ref, bpool_ref, pscale_ref, pool_o_ref)

        @pl.when(s_idx == pl.num_programs(1) - 1)
        def _():
            utail_ref[...] = ubuf[0:HIST_PAD, :]


def _const_spec(a):
    nd = a.ndim
    return pl.BlockSpec(a.shape, lambda *_: (0,) * nd)


def _pre_prompt(x, mod_p, cosx, sinx, w):
    b, s, d = x.shape
    tm = PROMPT_TILE
    pw = w["pool_scale"].shape[1]
    consts_a = [w["norm1"], w["w_in"], w["q_a_norm"], w["w_q"], w["kv_a_norm"]]
    consts_b = [w["q_gain"], w["k_gain"], w["w_kv"], w["w_pool"], w["b_pool"], w["pool_scale"]]
    tok = lambda width: pl.BlockSpec((None, tm, width), lambda i, j: (i, j, 0))
    head = lambda width: pl.BlockSpec((None, N_HEADS, tm, width), lambda i, j: (i, 0, j, 0))
    mod = lambda col: pl.BlockSpec((None, 1, d), lambda i, j: (i, 0, col))
    rope = pl.BlockSpec((tm, LANES), lambda i, j: (j, 0))
    return pl.pallas_call(
        functools.partial(_pre_kernel, sample=False, tm=tm),
        out_shape=(
            jax.ShapeDtypeStruct((b, N_HEADS, s, HEAD_PAD), BF16),
            jax.ShapeDtypeStruct((b, N_HEADS, s, HEAD_PAD), BF16),
            jax.ShapeDtypeStruct((b, N_HEADS, s, V_HEAD), BF16),
            jax.ShapeDtypeStruct((b, s, KV_LORA), F32),
            jax.ShapeDtypeStruct((b, s, QK_ROPE), F32),
            jax.ShapeDtypeStruct((b, s, pw), BF16),
            jax.ShapeDtypeStruct((b, HIST_PAD, pw), F32),
        ),
        grid=(b, s // tm),
        in_specs=[tok(d), mod(0), mod(1)] + [_const_spec(a) for a in consts_a] + [rope, rope]
        + [_const_spec(a) for a in consts_b],
        out_specs=(head(HEAD_PAD), head(HEAD_PAD), head(V_HEAD), tok(KV_LORA), tok(QK_ROPE), tok(pw),
                   pl.BlockSpec((None, HIST_PAD, pw), lambda i, j: (i, 0, 0))),
        scratch_shapes=[pltpu.VMEM((tm + HIST_PAD, pw), F32)]
        + [pltpu.VMEM((tm + HIST_PAD, pw // len(POOL_WINDOWS)), F32)] * 2,
        compiler_params=_params(("arbitrary", "arbitrary")),
        name="pre_prompt",
    )(x, mod_p, mod_p, *consts_a, cosx, sinx, *consts_b)


def _pre_sample(x, mod_s, cosx, sinx, w):
    n, d = x.shape
    bd = mod_s.shape[0]
    pw = w["pool_scale"].shape[1]
    consts = [w["norm1"], w["w_in"], w["q_a_norm"], w["w_q"], w["kv_a_norm"]]
    mod = lambda col: pl.BlockSpec((bd, d), lambda i: (0, col))
    full = lambda *shape: pl.BlockSpec(shape, lambda i: (0,) * len(shape))
    return pl.pallas_call(
        functools.partial(_pre_kernel, sample=True, tm=n),
        out_shape=(
            jax.ShapeDtypeStruct((N_HEADS, n, KV_LORA), BF16),
            jax.ShapeDtypeStruct((N_HEADS, n, LANES), BF16),
            jax.ShapeDtypeStruct((n, KV_LORA), F32),
            jax.ShapeDtypeStruct((n, QK_ROPE), F32),
            jax.ShapeDtypeStruct((n, pw), F32),
        ),
        grid=(1,),
        in_specs=[full(n, d), mod(0), mod(1)] + [_const_spec(a) for a in consts]
        + [full(n, LANES), full(n, LANES), _const_spec(w["q_gain"]), _const_spec(w["k_gain"]), _const_spec(w["w_kt"])],
        out_specs=(full(N_HEADS, n, KV_LORA), full(N_HEADS, n, LANES), full(n, KV_LORA), full(n, QK_ROPE),
                   full(n, pw)),
        compiler_params=_params(("arbitrary",)),
        name="pre_sample",
    )(x, mod_s, mod_s, *consts, cosx, sinx, w["q_gain"], w["k_gain"], w["w_kt"])


def _attn_kernel(q_ref, k_ref, v_ref, o_ref, *, seq, tile):
    n = seq // tile
    row = lax.broadcasted_iota(jnp.int32, (tile, tile), 0)
    col = lax.broadcasted_iota(jnp.int32, (tile, tile), 1)
    for qi in range(n):
        q = q_ref[qi * tile:(qi + 1) * tile, :]
        m = jnp.full((tile, 1), -jnp.inf, F32)
        l = jnp.zeros((tile, 1), F32)
        acc = jnp.zeros((tile, V_HEAD), F32)
        for ki in range(qi + 1):
            k = k_ref[ki * tile:(ki + 1) * tile, :]
            s = lax.dot_general(q, k, _NT, preferred_element_type=F32)
            if ki == qi:
                s = jnp.where(col <= row, s, -jnp.inf)
            m_new = jnp.maximum(m, jnp.max(s, axis=-1, keepdims=True))
            alpha = jnp.exp(m - m_new)
            p = jnp.exp(s - m_new)
            l = alpha * l + jnp.sum(p, axis=-1, keepdims=True)
            acc = alpha * acc + jnp.dot(p.astype(BF16), v_ref[ki * tile:(ki + 1) * tile, :],
                                        preferred_element_type=F32)
            m = m_new
        o_ref[qi * tile:(qi + 1) * tile, :] = (acc / l).astype(BF16)


def _attention_prompt(q, k, v):
    b, h, s, _ = q.shape
    head = lambda width: pl.BlockSpec((None, None, s, width), lambda i, j: (i, j, 0, 0))
    return pl.pallas_call(
        functools.partial(_attn_kernel, seq=s, tile=min(ATTN_TILE, s)),
        out_shape=jax.ShapeDtypeStruct((b, s, h * V_HEAD), BF16),
        grid=(b, h),
        in_specs=[head(HEAD_PAD), head(HEAD_PAD), head(V_HEAD)],
        out_specs=pl.BlockSpec((None, s, V_HEAD), lambda i, j: (i, 0, j)),
        compiler_params=_params(("arbitrary", "arbitrary")),
        name="attention_prompt",
    )(q, k, v)


def _sattn_kernel(pt_ref, q_ref, qr_ref, latn_ref, krn_ref, wkt_ref, clat_hbm, ckr_hbm, ctx_ref,
                  lat_buf, kr_buf, sem, lhs, s_even, s_odd, *, layer, n_seq, n_pages, page, n_new):
    b = pl.program_id(0)
    n_chunks = n_pages // CHUNK_PAGES
    per_stream = (n_seq // SEQ_STREAMS) * n_chunks
    g0 = b * n_chunks
    streams = range(SEQ_STREAMS)
    nq = q_ref.shape[1]
    n_k = wkt_ref.shape[0]
    s_bufs = (s_even, s_odd)

    def slot_of(g):
        return lax.rem(g, CACHE_SLOTS)

    def copies(st, g):
        slot = slot_of(g)
        out = []
        for p in range(CHUNK_PAGES):
            phys = pt_ref[(st * per_stream + g) * CHUNK_PAGES + p]
            dst = pl.ds(p * page, page)
            out.append(pltpu.make_async_copy(clat_hbm.at[layer, phys], lat_buf.at[st, slot, dst, :],
                                             sem.at[st, 0, slot]))
            out.append(pltpu.make_async_copy(ckr_hbm.at[layer, phys], kr_buf.at[st, slot, :, dst],
                                             sem.at[st, 1, slot]))
        return out

    def start(st, g):
        for cp in copies(st, g):
            cp.start()

    def wait(st, g):
        for cp in copies(st, g):
            cp.wait()

    def fetch_ahead(st, g):
        @pl.when(g + PREFETCH_CHUNKS < per_stream)
        def _():
            start(st, g + PREFETCH_CHUNKS)

    @pl.when(b == 0)
    def _():
        for st in streams:
            lhs[st, 0:n_k, :] = wkt_ref[...]
            for g in range(min(PREFETCH_CHUNKS, per_stream)):
                start(st, g)

    for st in streams:
        lhs[st, n_k:n_k + nq, :] = q_ref[st]

    def scores(st, lat, krt):
        t = lat.shape[0]
        big = lax.dot_general(lhs[st], lat.astype(BF16), _NT, preferred_element_type=F32)
        kn = big[0:n_k]
        ssq = jnp.sum((kn * kn).reshape(N_HEADS, QK_NOPE, t), axis=1)
        ssq = ssq + jnp.sum(krt * krt, axis=0, keepdims=True)
        r = lax.rsqrt(ssq * (1.0 / QK_HEAD) + EPS)
        sr = jnp.dot(qr_ref[st, :, 0:QK_ROPE], krt.astype(BF16), preferred_element_type=F32)
        return (big[n_k:] + sr) * jnp.tile(r, (nq // N_HEADS, 1))

    def update(s, lat, carry):
        m, l, acc = carry
        m_new = jnp.maximum(m, jnp.max(s, axis=-1, keepdims=True))
        alpha = jnp.exp(m - m_new)
        p = jnp.exp(s - m_new)
        l = alpha * l + jnp.sum(p, axis=-1, keepdims=True)
        acc = alpha * acc + jnp.dot(p.astype(BF16), lat.astype(BF16), preferred_element_type=F32)
        return m_new, l, acc

    for st in streams:
        wait(st, g0)
        fetch_ahead(st, g0)

    pad = LANES - latn_ref.shape[1]
    key = lax.broadcasted_iota(jnp.int32, (nq, LANES), 1)
    tok = lax.broadcasted_iota(jnp.int32, (nq, LANES), 0) // N_HEADS
    carries = []
    for st in streams:
        lat_new = jnp.concatenate([latn_ref[st], jnp.zeros((pad, KV_LORA), F32)], axis=0)
        s_new = jnp.where((key <= tok) & (key < n_new), scores(st, lat_new, krn_ref[st]), -jnp.inf)
        carry = (jnp.full((nq, 1), -jnp.inf, F32), jnp.zeros((nq, 1), F32), jnp.zeros((nq, KV_LORA), F32))
        carries.append(update(s_new, lat_new, carry))

    for st in streams:
        s_even[st] = scores(st, lat_buf[st, slot_of(g0)], kr_buf[st, slot_of(g0)])

    def enter(c):
        for st in streams:
            wait(st, g0 + c)
            fetch_ahead(st, g0 + c)

    def step(c, parity, carries):
        g = g0 + c
        out = []
        for st in streams:
            s_bufs[parity][st] = scores(st, lat_buf[st, slot_of(g)], kr_buf[st, slot_of(g)])
            out.append(update(s_bufs[1 - parity][st], lat_buf[st, slot_of(g - 1)], carries[st]))
        return tuple(out)

    def two_steps(i, carries):
        c = 1 + 2 * i
        enter(c)
        enter(c + 1)
        return step(c + 1, 0, step(c, 1, carries))

    carries = lax.fori_loop(0, (n_chunks - 1) // 2, two_steps, tuple(carries))
    if n_chunks % 2 == 0:
        enter(n_chunks - 1)
        carries = step(n_chunks - 1, 1, carries)
    for st in streams:
        m, l, acc = update(s_bufs[(n_chunks - 1) % 2][st], lat_buf[st, slot_of(g0 + n_chunks - 1)], carries[st])
        ctx_ref[st] = acc / l


def _attention_sample(page_table, qabs, qr, lat_new, krt_new, w_kt, cache_lat, cache_krt, layer, n_new):
    bd, nq, _ = qabs.shape
    n_pages = page_table.shape[1]
    page = cache_lat.shape[2]
    assert n_pages % CHUNK_PAGES == 0 and n_pages // CHUNK_PAGES >= 2 and bd % SEQ_STREAMS == 0
    rows = CHUNK_PAGES * page
    n_k = w_kt.shape[0]
    pt = page_table.reshape(bd // SEQ_STREAMS, SEQ_STREAMS, n_pages).transpose(1, 0, 2).reshape(-1)
    per_step = lambda r, c: pl.BlockSpec((SEQ_STREAMS, r, c), lambda i, pt: (i, 0, 0))
    grid_spec = pltpu.PrefetchScalarGridSpec(
        num_scalar_prefetch=1,
        grid=(bd // SEQ_STREAMS,),
        in_specs=[
            per_step(nq, KV_LORA), per_step(nq, LANES), per_step(SUBLANES, KV_LORA), per_step(QK_ROPE, LANES),
            pl.BlockSpec((n_k, KV_LORA), lambda i, pt: (0, 0)),
            pl.BlockSpec(memory_space=pl.ANY), pl.BlockSpec(memory_space=pl.ANY),
        ],
        out_specs=per_step(nq, KV_LORA),
        scratch_shapes=[
            pltpu.VMEM((SEQ_STREAMS, CACHE_SLOTS, rows, KV_LORA), F32),
            pltpu.VMEM((SEQ_STREAMS, CACHE_SLOTS, QK_ROPE, rows), F32),
            pltpu.SemaphoreType.DMA((SEQ_STREAMS, 2, CACHE_SLOTS)),
            pltpu.VMEM((SEQ_STREAMS, n_k + nq, KV_LORA), BF16),
            pltpu.VMEM((SEQ_STREAMS, nq, rows), F32),
            pltpu.VMEM((SEQ_STREAMS, nq, rows), F32),
        ],
    )
    return pl.pallas_call(
        functools.partial(_sattn_kernel, layer=layer, n_seq=bd, n_pages=n_pages, page=page, n_new=n_new),
        out_shape=jax.ShapeDtypeStruct((bd, nq, KV_LORA), F32),
        grid_spec=grid_spec,
        compiler_params=_params(("arbitrary",)),
        name="attention_sample",
    )(pt, qabs, qr, lat_new, krt_new, w_kt, cache_lat, cache_krt)


def _ctx_to_heads_kernel(ctx_ref, wv_ref, o_ref):
    n_tok, n_head, bd, _ = ctx_ref.shape
    for t in range(n_tok):
        for hd in range(n_head):
            o = jnp.dot(ctx_ref[t, hd].astype(BF16), wv_ref[hd], preferred_element_type=F32)
            o_ref[t * bd:(t + 1) * bd, hd * V_HEAD:(hd + 1) * V_HEAD] = o.astype(BF16)


def _ctx_to_heads(ctx, w_v):
    n_tok, n_head, bd, _ = ctx.shape
    return pl.pallas_call(
        _ctx_to_heads_kernel,
        out_shape=jax.ShapeDtypeStruct((n_tok * bd, n_head * V_HEAD), BF16),
        grid=(1,),
        in_specs=[_const_spec(ctx), _const_spec(w_v)],
        out_specs=pl.BlockSpec((n_tok * bd, n_head * V_HEAD), lambda i: (0, 0)),
        compiler_params=_params(("arbitrary",)),
        name="ctx_to_heads",
    )(ctx, w_v)


def _pool_sample_kernel(u_ref, hist_ref, wpool_ref, bpool_ref, pscale_ref, o_ref, *, first_pos):
    n_tok, bd, pw = u_ref.shape
    n_hist = hist_ref.shape[0]
    pg = pw // len(POOL_WINDOWS)

    def row(i, c0):
        return hist_ref[i, :, c0:c0 + pg] if i < n_hist else u_ref[i - n_hist, :, c0:c0 + pg]

    for t in range(n_tok):
        for g, w in enumerate(POOL_WINDOWS):
            c0 = g * pg
            ug = u_ref[t, :, c0:c0 + pg]
            acc = ug
            for j in range(1, w):
                acc = acc + row(n_hist + t - j, c0)
            d = acc / float(min(first_pos + t + 1, w)) - ug
            o = jnp.dot(d.astype(BF16), wpool_ref[g], preferred_element_type=F32) + bpool_ref[:, c0:c0 + pg]
            o_ref[t * bd:(t + 1) * bd, c0:c0 + pg] = (o * pscale_ref[:, c0:c0 + pg]).astype(BF16)


def _pool_sample(u, hist, w, first_pos):
    n_tok, bd, pw = u.shape
    args = (u, hist, w["w_pool"], w["b_pool"], w["pool_scale"])
    return pl.pallas_call(
        functools.partial(_pool_sample_kernel, first_pos=first_pos),
        out_shape=jax.ShapeDtypeStruct((n_tok * bd, pw), BF16),
        grid=(1,),
        in_specs=[_const_spec(a) for a in args],
        out_specs=pl.BlockSpec((n_tok * bd, pw), lambda i: (0, 0)),
        compiler_params=_params(("arbitrary",)),
        name="pool_sample",
    )(*args)


def _mix_kernel(x_ref, a_ref, p_ref, wout_ref, g1_ref, sh2_ref, sc2_ref, norm2_ref, x1_ref, h2_ref):
    tm = x_ref.shape[0]
    mixed = jnp.concatenate([a_ref[...], p_ref[...]], axis=1)
    mix = jnp.dot(mixed, wout_ref[...], preferred_element_type=F32)
    x1 = x_ref[...] + _rows(g1_ref[...], tm) * mix
    x1_ref[...] = x1
    h2 = _rms(x1, norm2_ref[...]) * (1.0 + _rows(sc2_ref[...], tm)) + _rows(sh2_ref[...], tm)
    h2_ref[...] = h2.astype(BF16)


def _ffn_kernel(h_ref, x1_ref, g2_ref, wg_hbm, wu_hbm, wd_hbm, y_ref, wg_buf, wu_buf, wd_buf, sem, *, n_steps):
    tf = wd_buf.shape[1]
    n_ff = wd_hbm.shape[0] // tf
    step = pl.program_id(0)
    g0 = step * n_ff
    g2 = _rows(g2_ref[...], y_ref.shape[0])

    def copies(j):
        slot = lax.rem(g0 + j, FFN_WEIGHT_SLOTS)
        cols = pl.ds((j % n_ff) * tf, tf)
        return (pltpu.make_async_copy(wg_hbm.at[:, cols], wg_buf.at[slot], sem.at[0, slot]),
                pltpu.make_async_copy(wu_hbm.at[:, cols], wu_buf.at[slot], sem.at[1, slot]),
                pltpu.make_async_copy(wd_hbm.at[cols, :], wd_buf.at[slot], sem.at[2, slot]))

    def start(j):
        for cp in copies(j):
            cp.start()

    def wait(j):
        for cp in copies(j):
            cp.wait()

    def if_more_steps(fn, j):
        if j < n_ff:
            fn(j)
        else:
            @pl.when(step + 1 < n_steps)
            def _():
                fn(j)

    @pl.when(step == 0)
    def _():
        start(0)
        start(1)
        wait(0)

    y_ref[...] = x1_ref[...]
    h = h_ref[...]
    for j in range(n_ff):
        slot = lax.rem(g0 + j, FFN_WEIGHT_SLOTS)
        gate = jnp.dot(h, wg_buf[slot], preferred_element_type=F32)
        up = jnp.dot(h, wu_buf[slot], preferred_element_type=F32)
        act = (gate * jax.nn.sigmoid(gate) * up).astype(BF16)
        if_more_steps(wait, j + 1)
        if_more_steps(start, j + 2)
        part = jnp.dot(act, wd_buf[slot], preferred_element_type=F32)
        y_ref[...] += g2 * part


def _layer_out_prompt(x, attn_o, pool_o, mod_p, w):
    b, s, d = x.shape
    tm = PROMPT_TILE
    tok = lambda width: pl.BlockSpec((None, tm, width), lambda i, j: (i, j, 0))
    mod = lambda col: pl.BlockSpec((None, 1, d), lambda i, j: (i, 0, col))
    x1, h2 = pl.pallas_call(
        _mix_kernel,
        out_shape=(jax.ShapeDtypeStruct((b, s, d), F32), jax.ShapeDtypeStruct((b, s, d), BF16)),
        grid=(b, s // tm),
        in_specs=[tok(d), tok(attn_o.shape[2]), tok(pool_o.shape[2]), _const_spec(w["w_out"]),
                  mod(2), mod(3), mod(4), _const_spec(w["norm2"])],
        out_specs=(tok(d), tok(d)),
        compiler_params=_params(("arbitrary", "arbitrary")),
        name="mix_prompt",
    )(x, attn_o, pool_o, w["w_out"], mod_p, mod_p, mod_p, w["norm2"])

    tm = FFN_TOKEN_TILE
    nst = s // tm
    tok = pl.BlockSpec((None, tm, d), lambda i: (i // nst, i % nst, 0))
    return _ffn(h2, x1, mod_p, w, grid_steps=b * nst, tok_spec=tok,
                g2_spec=pl.BlockSpec((None, 1, d), lambda i: (i // nst, 0, 5)), name="ffn_prompt")


def _ffn(h2, x1, mod, w, *, grid_steps, tok_spec, g2_spec, name):
    d = h2.shape[-1]
    tf = FFN_HIDDEN_TILE
    assert w["w_gate"].shape[1] % tf == 0
    hbm = pl.BlockSpec(memory_space=pl.ANY)
    return pl.pallas_call(
        functools.partial(_ffn_kernel, n_steps=grid_steps),
        out_shape=jax.ShapeDtypeStruct(x1.shape, F32),
        grid=(grid_steps,),
        in_specs=[tok_spec, tok_spec, g2_spec, hbm, hbm, hbm],
        out_specs=tok_spec,
        scratch_shapes=[pltpu.VMEM((FFN_WEIGHT_SLOTS, d, tf), BF16), pltpu.VMEM((FFN_WEIGHT_SLOTS, d, tf), BF16),
                        pltpu.VMEM((FFN_WEIGHT_SLOTS, tf, d), BF16),
                        pltpu.SemaphoreType.DMA((3, FFN_WEIGHT_SLOTS))],
        compiler_params=_params(("arbitrary",)),
        name=name,
    )(h2, x1, mod, w["w_gate"], w["w_up"], w["w_down"])


def _layer_out_sample(x, attn_o, pool_o, mod_s, w):
    n, d = x.shape
    bd = mod_s.shape[0]
    full = lambda *shape: pl.BlockSpec(shape, lambda *_: (0,) * len(shape))
    mod = lambda col: pl.BlockSpec((bd, d), lambda *_: (0, col))
    x1, h2 = pl.pallas_call(
        _mix_kernel,
        out_shape=(jax.ShapeDtypeStruct((n, d), F32), jax.ShapeDtypeStruct((n, d), BF16)),
        grid=(1,),
        in_specs=[full(n, d), full(*attn_o.shape), full(*pool_o.shape), _const_spec(w["w_out"]),
                  mod(2), mod(3), mod(4), _const_spec(w["norm2"])],
        out_specs=(full(n, d), full(n, d)),
        compiler_params=_params(("arbitrary",)),
        name="mix_sample",
    )(x, attn_o, pool_o, w["w_out"], mod_s, mod_s, mod_s, w["norm2"])

    return _ffn(h2, x1, mod_s, w, grid_steps=1, tok_spec=full(n, d), g2_spec=mod(5), name="ffn_sample")


def _rope_tables(pos):
    inv = 1.0 / (ROPE_THETA ** (jnp.arange(0, QK_ROPE, 2, dtype=F32) / QK_ROPE))
    ang = pos.astype(F32)[:, None] * inv[None, :]
    c, s = jnp.cos(ang), jnp.sin(ang)
    z = jnp.zeros((pos.shape[0], LANES - QK_ROPE), F32)
    return jnp.concatenate([c, c, z], axis=1), jnp.concatenate([-s, s, z], axis=1)


def _layer_weights(l, w_in, q_a_norm, w_q_b, kv_a_norm, w_kv_b, q_norm_nope, q_norm_rope, k_norm_nope, k_norm_rope,
                   w_pool, b_pool, pool_scale, w_out, norm1, norm2, w_gate, w_up, w_down):
    half = QK_ROPE // 2
    o_kr = Q_LORA + KV_LORA
    o_u = o_kr + QK_ROPE
    win = w_in[l]
    w_in_p = jnp.concatenate([win[:, :o_u], win[:, o_kr + half:o_u], win[:, o_kr:o_kr + half], win[:, o_u:]], axis=1)
    wq = w_q_b[l].reshape(Q_LORA, N_HEADS, QK_HEAD)
    w_q_p = jnp.concatenate([wq, wq[:, :, QK_NOPE + half:], wq[:, :, QK_NOPE:QK_NOPE + half]], axis=2)
    wkv = w_kv_b[l].reshape(KV_LORA, N_HEADS, QK_NOPE + V_HEAD)
    zpad = jnp.zeros((HEAD_PAD - QK_HEAD,), F32)
    row = lambda a: a.reshape(1, -1)
    return {
        "norm1": row(norm1[l]), "norm2": row(norm2[l]),
        "w_in": w_in_p.astype(BF16),
        "q_a_norm": row(q_a_norm[l]), "kv_a_norm": row(kv_a_norm[l]),
        "w_q": w_q_p.reshape(Q_LORA, N_HEADS * HEAD_PAD).astype(BF16),
        "w_kv": w_kv_b[l].astype(BF16),
        "w_kt": wkv[:, :, :QK_NOPE].reshape(KV_LORA, N_HEADS * QK_NOPE).T.astype(BF16),
        "w_v": wkv[:, :, QK_NOPE:].transpose(1, 0, 2).astype(BF16),
        "q_gain": row(jnp.concatenate([q_norm_nope[l], q_norm_rope[l], q_norm_rope[l], zpad]) * SCALE),
        "k_gain": row(jnp.concatenate([k_norm_nope[l], k_norm_rope[l], k_norm_rope[l], zpad])),
        "w_pool": w_pool[l].astype(BF16), "b_pool": row(b_pool[l]), "pool_scale": row(pool_scale[l]),
        "w_out": w_out[l].astype(BF16),
        "w_gate": w_gate[l].astype(BF16), "w_up": w_up[l].astype(BF16), "w_down": w_down[l].astype(BF16),
    }


def kernel(x_prompt, x_sample, cache_kv_latent, cache_k_rope, state_pool, page_table, c_prompt, c_sample,
           w_mod, b_mod, norm1, w_in, q_a_norm, w_q_b, kv_a_norm, w_kv_b,
           q_norm_nope, q_norm_rope, k_norm_nope, k_norm_rope,
           w_pool, b_pool, pool_scale, w_out, norm2, w_gate, w_up, w_down):
    b, s, d = x_prompt.shape
    bd, t, _ = x_sample.shape
    depth = w_mod.shape[0]
    past = page_table.shape[1] * cache_kv_latent.shape[2]
    cos_p, sin_p = _rope_tables(jnp.arange(s))
    cos_s, sin_s = _rope_tables(jnp.repeat(past + jnp.arange(t), bd))
    c_all = jnp.concatenate([c_prompt, c_sample], axis=0)
    cache_krt = jnp.swapaxes(cache_k_rope, 2, 3)

    yp = x_prompt
    ys = x_sample.transpose(1, 0, 2).reshape(t * bd, d)
    outs = [[] for _ in range(6)]
    for l in range(depth):
        w = _layer_weights(l, w_in, q_a_norm, w_q_b, kv_a_norm, w_kv_b, q_norm_nope, q_norm_rope, k_norm_nope,
                           k_norm_rope, w_pool, b_pool, pool_scale, w_out, norm1, norm2, w_gate, w_up, w_down)
        mod = _modulation(c_all, w_mod[l], b_mod[l].reshape(1, -1))
        mod_p = mod[:b].reshape(b, 1, -1)
        mod_s = mod[b:]

        q, k, v, lat_p, kr_p, pool_p, utail = _pre_prompt(yp, mod_p, cos_p, sin_p, w)
        attn_p = _attention_prompt(q, k, v)
        yp = _layer_out_prompt(yp, attn_p, pool_p, mod_p, w)

        qabs, qr, lat_s, kr_s, u_s = _pre_sample(ys, mod_s, cos_s, sin_s, w)
        per_seq = lambda a: a.reshape(N_HEADS, t, bd, -1).transpose(2, 1, 0, 3).reshape(bd, t * N_HEADS, -1)
        lat_s = lat_s.reshape(t, bd, -1).transpose(1, 0, 2)
        kr_s = kr_s.reshape(t, bd, -1).transpose(1, 0, 2)
        lat_new = jnp.pad(lat_s, ((0, 0), (0, SUBLANES - t), (0, 0)))
        krt_new = jnp.pad(kr_s.transpose(0, 2, 1), ((0, 0), (0, 0), (0, LANES - t)))
        ctx = _attention_sample(page_table, per_seq(qabs), per_seq(qr), lat_new, krt_new, w["w_kt"],
                                cache_kv_latent, cache_krt, l, t)
        attn_s = _ctx_to_heads(ctx.reshape(bd, t, N_HEADS, -1).transpose(1, 2, 0, 3), w["w_v"])
        u_s = u_s.reshape(t, bd, -1)
        pool_s = _pool_sample(u_s, state_pool[l].transpose(1, 0, 2), w, past)
        ys = _layer_out_sample(ys, attn_s, pool_s, mod_s, w)

        full_s = jnp.concatenate([state_pool[l], u_s.transpose(1, 0, 2)], axis=1)
        for dst, val in zip(outs, (lat_p, kr_p, utail[:, HIST_PAD - POOL_HIST:], lat_s, kr_s,
                                   full_s[:, -POOL_HIST:])):
            dst.append(val)

    ys = ys.reshape(t, bd, d).transpose(1, 0, 2)
    return (yp, ys) + tuple(jnp.stack(o) for o in outs)
```

```python
import functools

import jax
import jax.numpy as jnp
from jax import lax
from jax.experimental import pallas as pl
from jax.experimental.pallas import tpu as pltpu

F32 = jnp.float32
BF16 = jnp.bfloat16

N_HEADS = 8
QK_NOPE = 128
QK_ROPE = 64
QK_HEAD = QK_NOPE + QK_ROPE
V_HEAD = 128
Q_LORA = 512
KV_LORA = 256
POOL_WINDOWS = (2, 4, 8, 16)
POOL_HIST = max(POOL_WINDOWS) - 1
ROPE_THETA = 10000.0
EPS = 1e-6
SCALE = QK_HEAD ** -0.5

LANES = 128
SUBLANES = 8
HEAD_PAD = 2 * LANES
assert all(w & (w - 1) == 0 for w in POOL_WINDOWS)
HIST_PAD = SUBLANES * (max(POOL_WINDOWS).bit_length() - 1)
VMEM_LIMIT_BYTES = 56 * 1024 * 1024

PROMPT_TILE = 512
ATTN_TILE = 512
FFN_TOKEN_TILE = 512
FFN_HIDDEN_TILE = 512
FFN_WEIGHT_SLOTS = 4
CHUNK_PAGES = 16
PREFETCH_CHUNKS = 3
CACHE_SLOTS = PREFETCH_CHUNKS + 3
SEQ_STREAMS = 2

_NT = (((1,), (1,)), ((), ()))


def _rms(x, w):
    return x * lax.rsqrt(jnp.mean(x * x, axis=-1, keepdims=True) + EPS) * w


def _rows(a, n):
    r = a.shape[0]
    if r == 1 or r == n:
        return a
    return jnp.tile(a, (n // r, 1))


def _rope(blk, cosx, sinx):
    return blk * cosx + pltpu.roll(blk, LANES // 2, axis=1) * sinx


def _params(semantics):
    return pltpu.CompilerParams(dimension_semantics=semantics, vmem_limit_bytes=VMEM_LIMIT_BYTES)


def _mod_kernel(c_ref, w_ref, b_ref, o_ref):
    c = c_ref[...]
    a = (c * jax.nn.sigmoid(c)).astype(BF16)
    o_ref[...] = jnp.dot(a, w_ref[...].astype(BF16), preferred_element_type=F32) + b_ref[...]


def _modulation(c_all, w_mod, b_mod):
    n, d = c_all.shape
    width = w_mod.shape[1]
    tn = 1024
    return pl.pallas_call(
        _mod_kernel,
        out_shape=jax.ShapeDtypeStruct((n, width), F32),
        grid=(width // tn,),
        in_specs=[
            pl.BlockSpec((n, d), lambda j: (0, 0)),
            pl.BlockSpec((d, tn), lambda j: (0, j)),
            pl.BlockSpec((1, tn), lambda j: (0, j)),
        ],
        out_specs=pl.BlockSpec((n, tn), lambda j: (0, j)),
        compiler_params=_params(("arbitrary",)),
        name="modulation",
    )(c_all, w_mod, b_mod)


def _window_sum(ubuf, stages, c0, pg, w, tm):
    n_stage = w.bit_length() - 1
    read = lambda lo, n: ubuf[lo:lo + n, c0:c0 + pg]
    for k in range(n_stage):
        s = 1 << k
        if k == n_stage - 1:
            return read(HIST_PAD, tm) + read(HIST_PAD - s, tm)
        lo = SUBLANES * (k + 1)
        n = HIST_PAD + tm - lo
        dst = stages[k % 2]
        dst[lo:lo + n, :] = read(lo, n) + read(lo - s, n)
        read = lambda lo, n, src=dst: src[lo:lo + n, :]


def _pool_prompt(u, ubuf, stages, s_idx, tm, wpool_ref, bpool_ref, pscale_ref, pool_o_ref):
    pw = u.shape[1]
    pg = pw // len(POOL_WINDOWS)
    ubuf[HIST_PAD:HIST_PAD + tm, :] = u
    pos = s_idx * tm + lax.broadcasted_iota(jnp.int32, (tm, 1), 0)
    for g, w in enumerate(POOL_WINDOWS):
        c0 = g * pg
        ug = u[:, c0:c0 + pg]
        acc = _window_sum(ubuf, stages, c0, pg, w, tm)
        cnt = jnp.minimum(pos + 1, w).astype(F32)
        d = acc / cnt - ug
        o = jnp.dot(d.astype(BF16), wpool_ref[g], preferred_element_type=F32) + bpool_ref[:, c0:c0 + pg]
        pool_o_ref[:, c0:c0 + pg] = (o * pscale_ref[:, c0:c0 + pg]).astype(BF16)
    ubuf[0:HIST_PAD, :] = ubuf[tm:tm + HIST_PAD, :]


def _pre_kernel(*refs, sample, tm):
    (x_ref, sh1_ref, sc1_ref, norm1_ref, win_ref, qan_ref, wq_ref, kvan_ref, cos_ref, sin_ref, qg_ref, kg_ref,
     *rest) = refs
    if sample:
        wkt_ref, qabs_ref, qr_ref, lat_ref, kr_ref, u_ref = rest
    else:
        (wkv_ref, wpool_ref, bpool_ref, pscale_ref,
         q_ref, k_ref, v_ref, lat_ref, kr_ref, pool_o_ref, utail_ref, ubuf, stage_a, stage_b) = rest
        s_idx = pl.program_id(1)

        @pl.when(s_idx == 0)
        def _():
            ubuf[0:HIST_PAD, :] = jnp.zeros((HIST_PAD, ubuf.shape[1]), F32)

    o_kv = Q_LORA
    o_kr = o_kv + KV_LORA
    o_u = o_kr + LANES
    x = x_ref[...]
    h = _rms(x, norm1_ref[...]) * (1.0 + _rows(sc1_ref[...], tm)) + _rows(sh1_ref[...], tm)
    h = h.astype(BF16)
    cosx = cos_ref[...]
    sinx = sin_ref[...]
    qg = qg_ref[...]
    kg = kg_ref[...]

    cq = jnp.dot(h, win_ref[:, 0:o_kv], preferred_element_type=F32)
    q = jnp.dot(_rms(cq, qan_ref[...]).astype(BF16), wq_ref[...], preferred_element_type=F32)
    for hd in range(N_HEADS):
        c0 = hd * HEAD_PAD
        qn = q[:, c0:c0 + QK_NOPE]
        qr = _rope(q[:, c0 + QK_NOPE:c0 + HEAD_PAD], cosx, sinx)
        r = lax.rsqrt(jnp.sum(qn * qn + qr * qr, axis=-1, keepdims=True) * (1.0 / QK_HEAD) + EPS)
        qn = qn * r * qg[:, 0:QK_NOPE]
        qr = qr * r * qg[:, QK_NOPE:]
        if sample:
            qk = (qn * kg[:, 0:QK_NOPE]).astype(BF16)
            qabs = jnp.dot(qk, wkt_ref[hd * QK_NOPE:(hd + 1) * QK_NOPE, :], preferred_element_type=F32)
            qabs_ref[hd] = qabs.astype(BF16)
            qr_ref[hd] = (qr * kg[:, QK_NOPE:]).astype(BF16)
        else:
            q_ref[hd] = jnp.concatenate([qn, qr], axis=-1).astype(BF16)

    zk = jnp.dot(h, win_ref[:, o_kv:o_u], preferred_element_type=F32)
    lat = _rms(zk[:, 0:KV_LORA], kvan_ref[...])
    lat_ref[...] = lat
    kr = _rope(zk[:, KV_LORA:], cosx, sinx)
    kr_ref[...] = kr[:, 0:QK_ROPE]
    if not sample:
        kv = jnp.dot(lat.astype(BF16), wkv_ref[...], preferred_element_type=F32)
        kr_sq = kr * kr
        for hd in range(N_HEADS):
            c0 = hd * HEAD_PAD
            kn = kv[:, c0:c0 + QK_NOPE]
            rk = lax.rsqrt(jnp.sum(kn * kn + kr_sq, axis=-1, keepdims=True) * (1.0 / QK_HEAD) + EPS)
            k_ref[hd] = jnp.concatenate([kn * rk * kg[:, 0:QK_NOPE], kr * rk * kg[:, QK_NOPE:]], axis=-1).astype(BF16)
            v_ref[hd] = kv[:, c0 + QK_NOPE:c0 + HEAD_PAD].astype(BF16)

    u = jnp.dot(h, win_ref[:, o_u:], preferred_element_type=F32)
    if sample:
        u_ref[...] = u
    else:
        _pool_prompt(u, ubuf, (stage_a, stage_b), s_idx, tm, wpool_ref, bpool_ref, pscale_ref, pool_o_ref)

        @pl.when(s_idx == pl.num_programs(1) - 1)
        def _():
            utail_ref[...] = ubuf[0:HIST_PAD, :]


def _const_spec(a):
    nd = a.ndim
    return pl.BlockSpec(a.shape, lambda *_: (0,) * nd)


def _pre_prompt(x, mod_p, cosx, sinx, w):
    b, s, d = x.shape
    tm = PROMPT_TILE
    pw = w["pool_scale"].shape[1]
    consts_a = [w["norm1"], w["w_in"], w["q_a_norm"], w["w_q"], w["kv_a_norm"]]
    consts_b = [w["q_gain"], w["k_gain"], w["w_kv"], w["w_pool"], w["b_pool"], w["pool_scale"]]
    tok = lambda width: pl.BlockSpec((None, tm, width), lambda i, j: (i, j, 0))
    head = lambda width: pl.BlockSpec((None, N_HEADS, tm, width), lambda i, j: (i, 0, j, 0))
    mod = lambda col: pl.BlockSpec((None, 1, d), lambda i, j: (i, 0, col))
    rope = pl.BlockSpec((tm, LANES), lambda i, j: (j, 0))
    return pl.pallas_call(
        functools.partial(_pre_kernel, sample=False, tm=tm),
        out_shape=(
            jax.ShapeDtypeStruct((b, N_HEADS, s, HEAD_PAD), BF16),
            jax.ShapeDtypeStruct((b, N_HEADS, s, HEAD_PAD), BF16),
            jax.ShapeDtypeStruct((b, N_HEADS, s, V_HEAD), BF16),
            jax.ShapeDtypeStruct((b, s, KV_LORA), F32),
            jax.ShapeDtypeStruct((b, s, QK_ROPE), F32),
            jax.ShapeDtypeStruct((b, s, pw), BF16),
            jax.ShapeDtypeStruct((b, HIST_PAD, pw), F32),
        ),
        grid=(b, s // tm),
        in_specs=[tok(d), mod(0), mod(1)] + [_const_spec(a) for a in consts_a] + [rope, rope]
        + [_const_spec(a) for a in consts_b],
        out_specs=(head(HEAD_PAD), head(HEAD_PAD), head(V_HEAD), tok(KV_LORA), tok(QK_ROPE), tok(pw),
                   pl.BlockSpec((None, HIST_PAD, pw), lambda i, j: (i, 0, 0))),
        scratch_shapes=[pltpu.VMEM((tm + HIST_PAD, pw), F32)]
        + [pltpu.VMEM((tm + HIST_PAD, pw // len(POOL_WINDOWS)), F32)] * 2,
        compiler_params=_params(("arbitrary", "arbitrary")),
        name="pre_prompt",
    )(x, mod_p, mod_p, *consts_a, cosx, sinx, *consts_b)


def _pre_sample(x, mod_s, cosx, sinx, w):
    n, d = x.shape
    bd = mod_s.shape[0]
    pw = w["pool_scale"].shape[1]
    consts = [w["norm1"], w["w_in"], w["q_a_norm"], w["w_q"], w["kv_a_norm"]]
    mod = lambda col: pl.BlockSpec((bd, d), lambda i: (0, col))
    full = lambda *shape: pl.BlockSpec(shape, lambda i: (0,) * len(shape))
    return pl.pallas_call(
        functools.partial(_pre_kernel, sample=True, tm=n),
        out_shape=(
            jax.ShapeDtypeStruct((N_HEADS, n, KV_LORA), BF16),
            jax.ShapeDtypeStruct((N_HEADS, n, LANES), BF16),
            jax.ShapeDtypeStruct((n, KV_LORA), F32),
            jax.ShapeDtypeStruct((n, QK_ROPE), F32),
            jax.ShapeDtypeStruct((n, pw), F32),
        ),
        grid=(1,),
        in_specs=[full(n, d), mod(0), mod(1)] + [_const_spec(a) for a in consts]
        + [full(n, LANES), full(n, LANES), _const_spec(w["q_gain"]), _const_spec(w["k_gain"]), _const_spec(w["w_kt"])],
        out_specs=(full(N_HEADS, n, KV_LORA), full(N_HEADS, n, LANES), full(n, KV_LORA), full(n, QK_ROPE),
                   full(n, pw)),
        compiler_params=_params(("arbitrary",)),
        name="pre_sample",
    )(x, mod_s, mod_s, *consts, cosx, sinx, w["q_gain"], w["k_gain"], w["w_kt"])


def _attn_kernel(q_ref, k_ref, v_ref, o_ref, *, seq, tile):
    n = seq // tile
    row = lax.broadcasted_iota(jnp.int32, (tile, tile), 0)
    col = lax.broadcasted_iota(jnp.int32, (tile, tile), 1)
    for qi in range(n):
        q = q_ref[qi * tile:(qi + 1) * tile, :]
        m = jnp.full((tile, 1), -jnp.inf, F32)
        l = jnp.zeros((tile, 1), F32)
        acc = jnp.zeros((tile, V_HEAD), F32)
        for ki in range(qi + 1):
            k = k_ref[ki * tile:(ki + 1) * tile, :]
            s = lax.dot_general(q, k, _NT, preferred_element_type=F32)
            if ki == qi:
                s = jnp.where(col <= row, s, -jnp.inf)
            m_new = jnp.maximum(m, jnp.max(s, axis=-1, keepdims=True))
            alpha = jnp.exp(m - m_new)
            p = jnp.exp(s - m_new)
            l = alpha * l + jnp.sum(p, axis=-1, keepdims=True)
            acc = alpha * acc + jnp.dot(p.astype(BF16), v_ref[ki * tile:(ki + 1) * tile, :],
                                        preferred_element_type=F32)
            m = m_new
        o_ref[qi * tile:(qi + 1) * tile, :] = (acc / l).astype(BF16)


def _attention_prompt(q, k, v):
    b, h, s, _ = q.shape
    head = lambda width: pl.BlockSpec((None, None, s, width), lambda i, j: (i, j, 0, 0))
    return pl.pallas_call(
        functools.partial(_attn_kernel, seq=s, tile=min(ATTN_TILE, s)),
        out_shape=jax.ShapeDtypeStruct((b, s, h * V_HEAD), BF16),
        grid=(b, h),
        in_specs=[head(HEAD_PAD), head(HEAD_PAD), head(V_HEAD)],
        out_specs=pl.BlockSpec((None, s, V_HEAD), lambda i, j: (i, 0, j)),
        compiler_params=_params(("arbitrary", "arbitrary")),
        name="attention_prompt",
    )(q, k, v)


def _sattn_kernel(pt_ref, q_ref, qr_ref, latn_ref, krn_ref, wkt_ref, clat_hbm, ckr_hbm, ctx_ref,
                  lat_buf, kr_buf, sem, lhs, s_even, s_odd, *, layer, n_seq, n_pages, page, n_new):
    b = pl.program_id(0)
    n_chunks = n_pages // CHUNK_PAGES
    per_stream = (n_seq // SEQ_STREAMS) * n_chunks
    g0 = b * n_chunks
    streams = range(SEQ_STREAMS)
    nq = q_ref.shape[1]
    n_k = wkt_ref.shape[0]
    s_bufs = (s_even, s_odd)

    def slot_of(g):
        return lax.rem(g, CACHE_SLOTS)

    def copies(st, g):
        slot = slot_of(g)
        out = []
        for p in range(CHUNK_PAGES):
            phys = pt_ref[(st * per_stream + g) * CHUNK_PAGES + p]
            dst = pl.ds(p * page, page)
            out.append(pltpu.make_async_copy(clat_hbm.at[layer, phys], lat_buf.at[st, slot, dst, :],
                                             sem.at[st, 0, slot]))
            out.append(pltpu.make_async_copy(ckr_hbm.at[layer, phys], kr_buf.at[st, slot, :, dst],
                                             sem.at[st, 1, slot]))
        return out

    def start(st, g):
        for cp in copies(st, g):
            cp.start()

    def wait(st, g):
        for cp in copies(st, g):
            cp.wait()

    def fetch_ahead(st, g):
        @pl.when(g + PREFETCH_CHUNKS < per_stream)
        def _():
            start(st, g + PREFETCH_CHUNKS)

    @pl.when(b == 0)
    def _():
        for st in streams:
            lhs[st, 0:n_k, :] = wkt_ref[...]
            for g in range(min(PREFETCH_CHUNKS, per_stream)):
                start(st, g)

    for st in streams:
        lhs[st, n_k:n_k + nq, :] = q_ref[st]

    def scores(st, lat, krt):
        t = lat.shape[0]
        big = lax.dot_general(lhs[st], lat.astype(BF16), _NT, preferred_element_type=F32)
        kn = big[0:n_k]
        ssq = jnp.sum((kn * kn).reshape(N_HEADS, QK_NOPE, t), axis=1)
        ssq = ssq + jnp.sum(krt * krt, axis=0, keepdims=True)
        r = lax.rsqrt(ssq * (1.0 / QK_HEAD) + EPS)
        sr = jnp.dot(qr_ref[st, :, 0:QK_ROPE], krt.astype(BF16), preferred_element_type=F32)
        return (big[n_k:] + sr) * jnp.tile(r, (nq // N_HEADS, 1))

    def update(s, lat, carry):
        m, l, acc = carry
        m_new = jnp.maximum(m, jnp.max(s, axis=-1, keepdims=True))
        alpha = jnp.exp(m - m_new)
        p = jnp.exp(s - m_new)
        l = alpha * l + jnp.sum(p, axis=-1, keepdims=True)
        acc = alpha * acc + jnp.dot(p.astype(BF16), lat.astype(BF16), preferred_element_type=F32)
        return m_new, l, acc

    for st in streams:
        wait(st, g0)
        fetch_ahead(st, g0)

    pad = LANES - latn_ref.shape[1]
    key = lax.broadcasted_iota(jnp.int32, (nq, LANES), 1)
    tok = lax.broadcasted_iota(jnp.int32, (nq, LANES), 0) // N_HEADS
    carries = []
    for st in streams:
        lat_new = jnp.concatenate([latn_ref[st], jnp.zeros((pad, KV_LORA), F32)], axis=0)
        s_new = jnp.where((key <= tok) & (key < n_new), scores(st, lat_new, krn_ref[st]), -jnp.inf)
        carry = (jnp.full((nq, 1), -jnp.inf, F32), jnp.zeros((nq, 1), F32), jnp.zeros((nq, KV_LORA), F32))
        carries.append(update(s_new, lat_new, carry))

    for st in streams:
        s_even[st] = scores(st, lat_buf[st, slot_of(g0)], kr_buf[st, slot_of(g0)])

    def enter(c):
        for st in streams:
            wait(st, g0 + c)
            fetch_ahead(st, g0 + c)

    def step(c, parity, carries):
        g = g0 + c
        out = []
        for st in streams:
            s_bufs[parity][st] = scores(st, lat_buf[st, slot_of(g)], kr_buf[st, slot_of(g)])
            out.append(update(s_bufs[1 - parity][st], lat_buf[st, slot_of(g - 1)], carries[st]))
        return tuple(out)

    def two_steps(i, carries):
        c = 1 + 2 * i
        enter(c)
        enter(c + 1)
        return step(c + 1, 0, step(c, 1, carries))

    carries = lax.fori_loop(0, (n_chunks - 1) // 2, two_steps, tuple(carries))
    if n_chunks % 2 == 0:
        enter(n_chunks - 1)
        carries = step(n_chunks - 1, 1, carries)
    for st in streams:
        m, l, acc = update(s_bufs[(n_chunks - 1) % 2][st], lat_buf[st, slot_of(g0 + n_chunks - 1)], carries[st])
        ctx_ref[st] = acc / l


def _attention_sample(page_table, qabs, qr, lat_new, krt_new, w_kt, cache_lat, cache_krt, layer, n_new):
    bd, nq, _ = qabs.shape
    n_pages = page_table.shape[1]
    page = cache_lat.shape[2]
    assert n_pages % CHUNK_PAGES == 0 and n_pages // CHUNK_PAGES >= 2 and bd % SEQ_STREAMS == 0
    rows = CHUNK_PAGES * page
    n_k = w_kt.shape[0]
    pt = page_table.reshape(bd // SEQ_STREAMS, SEQ_STREAMS, n_pages).transpose(1, 0, 2).reshape(-1)
    per_step = lambda r, c: pl.BlockSpec((SEQ_STREAMS, r, c), lambda i, pt: (i, 0, 0))
    grid_spec = pltpu.PrefetchScalarGridSpec(
        num_scalar_prefetch=1,
        grid=(bd // SEQ_STREAMS,),
        in_specs=[
            per_step(nq, KV_LORA), per_step(nq, LANES), per_step(SUBLANES, KV_LORA), per_step(QK_ROPE, LANES),
            pl.BlockSpec((n_k, KV_LORA), lambda i, pt: (0, 0)),
            pl.BlockSpec(memory_space=pl.ANY), pl.BlockSpec(memory_space=pl.ANY),
        ],
        out_specs=per_step(nq, KV_LORA),
        scratch_shapes=[
            pltpu.VMEM((SEQ_STREAMS, CACHE_SLOTS, rows, KV_LORA), F32),
            pltpu.VMEM((SEQ_STREAMS, CACHE_SLOTS, QK_ROPE, rows), F32),
            pltpu.SemaphoreType.DMA((SEQ_STREAMS, 2, CACHE_SLOTS)),
            pltpu.VMEM((SEQ_STREAMS, n_k + nq, KV_LORA), BF16),
            pltpu.VMEM((SEQ_STREAMS, nq, rows), F32),
            pltpu.VMEM((SEQ_STREAMS, nq, rows), F32),
        ],
    )
    return pl.pallas_call(
        functools.partial(_sattn_kernel, layer=layer, n_seq=bd, n_pages=n_pages, page=page, n_new=n_new),
        out_shape=jax.ShapeDtypeStruct((bd, nq, KV_LORA), F32),
        grid_spec=grid_spec,
        compiler_params=_params(("arbitrary",)),
        name="attention_sample",
    )(pt, qabs, qr, lat_new, krt_new, w_kt, cache_lat, cache_krt)


def _ctx_to_heads_kernel(ctx_ref, wv_ref, o_ref):
    n_tok, n_head, bd, _ = ctx_ref.shape
    for t in range(n_tok):
        for hd in range(n_head):
            o = jnp.dot(ctx_ref[t, hd].astype(BF16), wv_ref[hd], preferred_element_type=F32)
            o_ref[t * bd:(t + 1) * bd, hd * V_HEAD:(hd + 1) * V_HEAD] = o.astype(BF16)


def _ctx_to_heads(ctx, w_v):
    n_tok, n_head, bd, _ = ctx.shape
    return pl.pallas_call(
        _ctx_to_heads_kernel,
        out_shape=jax.ShapeDtypeStruct((n_tok * bd, n_head * V_HEAD), BF16),
        grid=(1,),
        in_specs=[_const_spec(ctx), _const_spec(w_v)],
        out_specs=pl.BlockSpec((n_tok * bd, n_head * V_HEAD), lambda i: (0, 0)),
        compiler_params=_params(("arbitrary",)),
        name="ctx_to_heads",
    )(ctx, w_v)


def _pool_sample_kernel(u_ref, hist_ref, wpool_ref, bpool_ref, pscale_ref, o_ref, *, first_pos):
    n_tok, bd, pw = u_ref.shape
    n_hist = hist_ref.shape[0]
    pg = pw // len(POOL_WINDOWS)

    def row(i, c0):
        return hist_ref[i, :, c0:c0 + pg] if i < n_hist else u_ref[i - n_hist, :, c0:c0 + pg]

    for t in range(n_tok):
        for g, w in enumerate(POOL_WINDOWS):
            c0 = g * pg
            ug = u_ref[t, :, c0:c0 + pg]
            acc = ug
            for j in range(1, w):
                acc = acc + row(n_hist + t - j, c0)
            d = acc / float(min(first_pos + t + 1, w)) - ug
            o = jnp.dot(d.astype(BF16), wpool_ref[g], preferred_element_type=F32) + bpool_ref[:, c0:c0 + pg]
            o_ref[t * bd:(t + 1) * bd, c0:c0 + pg] = (o * pscale_ref[:, c0:c0 + pg]).astype(BF16)


def _pool_sample(u, hist, w, first_pos):
    n_tok, bd, pw = u.shape
    args = (u, hist, w["w_pool"], w["b_pool"], w["pool_scale"])
    return pl.pallas_call(
        functools.partial(_pool_sample_kernel, first_pos=first_pos),
        out_shape=jax.ShapeDtypeStruct((n_tok * bd, pw), BF16),
        grid=(1,),
        in_specs=[_const_spec(a) for a in args],
        out_specs=pl.BlockSpec((n_tok * bd, pw), lambda i: (0, 0)),
        compiler_params=_params(("arbitrary",)),
        name="pool_sample",
    )(*args)


def _mix_kernel(x_ref, a_ref, p_ref, wout_ref, g1_ref, sh2_ref, sc2_ref, norm2_ref, x1_ref, h2_ref):
    tm = x_ref.shape[0]
    mixed = jnp.concatenate([a_ref[...], p_ref[...]], axis=1)
    mix = jnp.dot(mixed, wout_ref[...], preferred_element_type=F32)
    x1 = x_ref[...] + _rows(g1_ref[...], tm) * mix
    x1_ref[...] = x1
    h2 = _rms(x1, norm2_ref[...]) * (1.0 + _rows(sc2_ref[...], tm)) + _rows(sh2_ref[...], tm)
    h2_ref[...] = h2.astype(BF16)


def _ffn_kernel(h_ref, x1_ref, g2_ref, wg_hbm, wu_hbm, wd_hbm, y_ref, wg_buf, wu_buf, wd_buf, sem, *, n_steps):
    tf = wd_buf.shape[1]
    n_ff = wd_hbm.shape[0] // tf
    step = pl.program_id(0)
    g0 = step * n_ff
    g2 = _rows(g2_ref[...], y_ref.shape[0])

    def copies(j):
        slot = lax.rem(g0 + j, FFN_WEIGHT_SLOTS)
        cols = pl.ds((j % n_ff) * tf, tf)
        return (pltpu.make_async_copy(wg_hbm.at[:, cols], wg_buf.at[slot], sem.at[0, slot]),
                pltpu.make_async_copy(wu_hbm.at[:, cols], wu_buf.at[slot], sem.at[1, slot]),
                pltpu.make_async_copy(wd_hbm.at[cols, :], wd_buf.at[slot], sem.at[2, slot]))

    def start(j):
        for cp in copies(j):
            cp.start()

    def wait(j):
        for cp in copies(j):
            cp.wait()

    def if_more_steps(fn, j):
        if j < n_ff:
            fn(j)
        else:
            @pl.when(step + 1 < n_steps)
            def _():
                fn(j)

    @pl.when(step == 0)
    def _():
        for j in range(FFN_WEIGHT_SLOTS - 1):
            start(j)
        wait(0)

    y_ref[...] = x1_ref[...]
    h = h_ref[...]
    for j in range(n_ff):
        slot = lax.rem(g0 + j, FFN_WEIGHT_SLOTS)
        gate = jnp.dot(h, wg_buf[slot], preferred_element_type=F32)
        up = jnp.dot(h, wu_buf[slot], preferred_element_type=F32)
        act = (gate * jax.nn.sigmoid(gate) * up).astype(BF16)
        if_more_steps(wait, j + 1)
        if_more_steps(start, j + FFN_WEIGHT_SLOTS - 1)
        part = jnp.dot(act, wd_buf[slot], preferred_element_type=F32)
        y_ref[...] += g2 * part


def _layer_out_prompt(x, attn_o, pool_o, mod_p, w):
    b, s, d = x.shape
    tm = PROMPT_TILE
    tok = lambda width: pl.BlockSpec((None, tm, width), lambda i, j: (i, j, 0))
    mod = lambda col: pl.BlockSpec((None, 1, d), lambda i, j: (i, 0, col))
    x1, h2 = pl.pallas_call(
        _mix_kernel,
        out_shape=(jax.ShapeDtypeStruct((b, s, d), F32), jax.ShapeDtypeStruct((b, s, d), BF16)),
        grid=(b, s // tm),
        in_specs=[tok(d), tok(attn_o.shape[2]), tok(pool_o.shape[2]), _const_spec(w["w_out"]),
                  mod(2), mod(3), mod(4), _const_spec(w["norm2"])],
        out_specs=(tok(d), tok(d)),
        compiler_params=_params(("arbitrary", "arbitrary")),
        name="mix_prompt",
    )(x, attn_o, pool_o, w["w_out"], mod_p, mod_p, mod_p, w["norm2"])

    tm = FFN_TOKEN_TILE
    nst = s // tm
    tok = pl.BlockSpec((None, tm, d), lambda i: (i // nst, i % nst, 0))
    return _ffn(h2, x1, mod_p, w, grid_steps=b * nst, tok_spec=tok,
                g2_spec=pl.BlockSpec((None, 1, d), lambda i: (i // nst, 0, 5)), name="ffn_prompt")


def _ffn(h2, x1, mod, w, *, grid_steps, tok_spec, g2_spec, name):
    d = h2.shape[-1]
    tf = FFN_HIDDEN_TILE
    assert w["w_gate"].shape[1] % tf == 0
    hbm = pl.BlockSpec(memory_space=pl.ANY)
    return pl.pallas_call(
        functools.partial(_ffn_kernel, n_steps=grid_steps),
        out_shape=jax.ShapeDtypeStruct(x1.shape, F32),
        grid=(grid_steps,),
        in_specs=[tok_spec, tok_spec, g2_spec, hbm, hbm, hbm],
        out_specs=tok_spec,
        scratch_shapes=[pltpu.VMEM((FFN_WEIGHT_SLOTS, d, tf), BF16), pltpu.VMEM((FFN_WEIGHT_SLOTS, d, tf), BF16),
                        pltpu.VMEM((FFN_WEIGHT_SLOTS, tf, d), BF16),
                        pltpu.SemaphoreType.DMA((3, FFN_WEIGHT_SLOTS))],
        compiler_params=_params(("arbitrary",)),
        name=name,
    )(h2, x1, mod, w["w_gate"], w["w_up"], w["w_down"])


def _layer_out_sample(x, attn_o, pool_o, mod_s, w):
    n, d = x.shape
    bd = mod_s.shape[0]
    full = lambda *shape: pl.BlockSpec(shape, lambda *_: (0,) * len(shape))
    mod = lambda col: pl.BlockSpec((bd, d), lambda *_: (0, col))
    x1, h2 = pl.pallas_call(
        _mix_kernel,
        out_shape=(jax.ShapeDtypeStruct((n, d), F32), jax.ShapeDtypeStruct((n, d), BF16)),
        grid=(1,),
        in_specs=[full(n, d), full(*attn_o.shape), full(*pool_o.shape), _const_spec(w["w_out"]),
                  mod(2), mod(3), mod(4), _const_spec(w["norm2"])],
        out_specs=(full(n, d), full(n, d)),
        compiler_params=_params(("arbitrary",)),
        name="mix_sample",
    )(x, attn_o, pool_o, w["w_out"], mod_s, mod_s, mod_s, w["norm2"])

    return _ffn(h2, x1, mod_s, w, grid_steps=1, tok_spec=full(n, d), g2_spec=mod(5), name="ffn_sample")


def _rope_tables(pos):
    inv = 1.0 / (ROPE_THETA ** (jnp.arange(0, QK_ROPE, 2, dtype=F32) / QK_ROPE))
    ang = pos.astype(F32)[:, None] * inv[None, :]
    c, s = jnp.cos(ang), jnp.sin(ang)
    z = jnp.zeros((pos.shape[0], LANES - QK_ROPE), F32)
    return jnp.concatenate([c, c, z], axis=1), jnp.concatenate([-s, s, z], axis=1)


def _layer_weights(l, w_in, q_a_norm, w_q_b, kv_a_norm, w_kv_b, q_norm_nope, q_norm_rope, k_norm_nope, k_norm_rope,
                   w_pool, b_pool, pool_scale, w_out, norm1, norm2, w_gate, w_up, w_down):
    half = QK_ROPE // 2
    o_kr = Q_LORA + KV_LORA
    o_u = o_kr + QK_ROPE
    win = w_in[l]
    w_in_p = jnp.concatenate([win[:, :o_u], win[:, o_kr + half:o_u], win[:, o_kr:o_kr + half], win[:, o_u:]], axis=1)
    wq = w_q_b[l].reshape(Q_LORA, N_HEADS, QK_HEAD)
    w_q_p = jnp.concatenate([wq, wq[:, :, QK_NOPE + half:], wq[:, :, QK_NOPE:QK_NOPE + half]], axis=2)
    wkv = w_kv_b[l].reshape(KV_LORA, N_HEADS, QK_NOPE + V_HEAD)
    zpad = jnp.zeros((HEAD_PAD - QK_HEAD,), F32)
    row = lambda a: a.reshape(1, -1)
    return {
        "norm1": row(norm1[l]), "norm2": row(norm2[l]),
        "w_in": w_in_p.astype(BF16),
        "q_a_norm": row(q_a_norm[l]), "kv_a_norm": row(kv_a_norm[l]),
        "w_q": w_q_p.reshape(Q_LORA, N_HEADS * HEAD_PAD).astype(BF16),
        "w_kv": w_kv_b[l].astype(BF16),
        "w_kt": wkv[:, :, :QK_NOPE].reshape(KV_LORA, N_HEADS * QK_NOPE).T.astype(BF16),
        "w_v": wkv[:, :, QK_NOPE:].transpose(1, 0, 2).astype(BF16),
        "q_gain": row(jnp.concatenate([q_norm_nope[l], q_norm_rope[l], q_norm_rope[l], zpad]) * SCALE),
        "k_gain": row(jnp.concatenate([k_norm_nope[l], k_norm_rope[l], k_norm_rope[l], zpad])),
        "w_pool": w_pool[l].astype(BF16), "b_pool": row(b_pool[l]), "pool_scale": row(pool_scale[l]),
        "w_out": w_out[l].astype(BF16),
        "w_gate": w_gate[l].astype(BF16), "w_up": w_up[l].astype(BF16), "w_down": w_down[l].astype(BF16),
    }


def kernel(x_prompt, x_sample, cache_kv_latent, cache_k_rope, state_pool, page_table, c_prompt, c_sample,
           w_mod, b_mod, norm1, w_in, q_a_norm, w_q_b, kv_a_norm, w_kv_b,
           q_norm_nope, q_norm_rope, k_norm_nope, k_norm_rope,
           w_pool, b_pool, pool_scale, w_out, norm2, w_gate, w_up, w_down):
    b, s, d = x_prompt.shape
    bd, t, _ = x_sample.shape
    depth = w_mod.shape[0]
    past = page_table.shape[1] * cache_kv_latent.shape[2]
    cos_p, sin_p = _rope_tables(jnp.arange(s))
    cos_s, sin_s = _rope_tables(jnp.repeat(past + jnp.arange(t), bd))
    c_all = jnp.concatenate([c_prompt, c_sample], axis=0)
    cache_krt = jnp.swapaxes(cache_k_rope, 2, 3)

    yp = x_prompt
    ys = x_sample.transpose(1, 0, 2).reshape(t * bd, d)
    outs = [[] for _ in range(6)]
    for l in range(depth):
        w = _layer_weights(l, w_in, q_a_norm, w_q_b, kv_a_norm, w_kv_b, q_norm_nope, q_norm_rope, k_norm_nope,
                           k_norm_rope, w_pool, b_pool, pool_scale, w_out, norm1, norm2, w_gate, w_up, w_down)
        mod = _modulation(c_all, w_mod[l], b_mod[l].reshape(1, -1))
        mod_p = mod[:b].reshape(b, 1, -1)
        mod_s = mod[b:]

        q, k, v, lat_p, kr_p, pool_p, utail = _pre_prompt(yp, mod_p, cos_p, sin_p, w)
        attn_p = _attention_prompt(q, k, v)
        yp = _layer_out_prompt(yp, attn_p, pool_p, mod_p, w)

        qabs, qr, lat_s, kr_s, u_s = _pre_sample(ys, mod_s, cos_s, sin_s, w)
        per_seq = lambda a: a.reshape(N_HEADS, t, bd, -1).transpose(2, 1, 0, 3).reshape(bd, t * N_HEADS, -1)
        lat_s = lat_s.reshape(t, bd, -1).transpose(1, 0, 2)
        kr_s = kr_s.reshape(t, bd, -1).transpose(1, 0, 2)
        lat_new = jnp.pad(lat_s, ((0, 0), (0, SUBLANES - t), (0, 0)))
        krt_new = jnp.pad(kr_s.transpose(0, 2, 1), ((0, 0), (0, 0), (0, LANES - t)))
        ctx = _attention_sample(page_table, per_seq(qabs), per_seq(qr), lat_new, krt_new, w["w_kt"],
                                cache_kv_latent, cache_krt, l, t)
        attn_s = _ctx_to_heads(ctx.reshape(bd, t, N_HEADS, -1).transpose(1, 2, 0, 3), w["w_v"])
        u_s = u_s.reshape(t, bd, -1)
        pool_s = _pool_sample(u_s, state_pool[l].transpose(1, 0, 2), w, past)
        ys = _layer_out_sample(ys, attn_s, pool_s, mod_s, w)

        full_s = jnp.concatenate([state_pool[l], u_s.transpose(1, 0, 2)], axis=1)
        for dst, val in zip(outs, (lat_p, kr_p, utail[:, HIST_PAD - POOL_HIST:], lat_s, kr_s,
                                   full_s[:, -POOL_HIST:])):
            dst.append(val)

    ys = ys.reshape(t, bd, d).transpose(1, 0, 2)
    return (yp, ys) + tuple(jnp.stack(o) for o in outs)
```

```python
import functools

import jax
import jax.numpy as jnp
from jax import lax
from jax.experimental import pallas as pl
from jax.experimental.pallas import tpu as pltpu

F32 = jnp.float32
BF16 = jnp.bfloat16

N_HEADS = 8
QK_NOPE = 128
QK_ROPE = 64
QK_HEAD = QK_NOPE + QK_ROPE
V_HEAD = 128
Q_LORA = 512
KV_LORA = 256
POOL_WINDOWS = (2, 4, 8, 16)
POOL_HIST = max(POOL_WINDOWS) - 1
ROPE_THETA = 10000.0
EPS = 1e-6
SCALE = QK_HEAD ** -0.5

LANES = 128
SUBLANES = 8
HEAD_PAD = 2 * LANES
assert all(w & (w - 1) == 0 for w in POOL_WINDOWS)
HIST_PAD = SUBLANES * (max(POOL_WINDOWS).bit_length() - 1)
VMEM_LIMIT_BYTES = 56 * 1024 * 1024

MOD_COL_TILE = 1024
PROMPT_TILE = 512
ATTN_TILE = 512
FFN_TOKEN_TILE = 512
FFN_HIDDEN_TILE = 512
FFN_WEIGHT_SLOTS = 4
CHUNK_PAGES = 16
PREFETCH_CHUNKS = 4
CACHE_SLOTS = PREFETCH_CHUNKS + 3
SEQ_STREAMS = 2

_NT = (((1,), (1,)), ((), ()))


def _rms(x, w):
    return x * lax.rsqrt(jnp.mean(x * x, axis=-1, keepdims=True) + EPS) * w


def _rows(a, n):
    r = a.shape[0]
    if r == 1 or r == n:
        return a
    return jnp.tile(a, (n // r, 1))


def _rope(blk, cosx, sinx):
    return blk * cosx + pltpu.roll(blk, LANES // 2, axis=1) * sinx


def _params(semantics):
    return pltpu.CompilerParams(dimension_semantics=semantics, vmem_limit_bytes=VMEM_LIMIT_BYTES)


def _mod_kernel(c_ref, w_ref, b_ref, o_ref):
    c = c_ref[...]
    a = (c * jax.nn.sigmoid(c)).astype(BF16)
    o_ref[...] = jnp.dot(a, w_ref[...].astype(BF16), preferred_element_type=F32) + b_ref[...]


def _modulation(c_all, w_mod, b_mod):
    n, d = c_all.shape
    width = w_mod.shape[1]
    tn = MOD_COL_TILE
    assert width % tn == 0
    return pl.pallas_call(
        _mod_kernel,
        out_shape=jax.ShapeDtypeStruct((n, width), F32),
        grid=(width // tn,),
        in_specs=[
            pl.BlockSpec((n, d), lambda j: (0, 0)),
            pl.BlockSpec((d, tn), lambda j: (0, j)),
            pl.BlockSpec((1, tn), lambda j: (0, j)),
        ],
        out_specs=pl.BlockSpec((n, tn), lambda j: (0, j)),
        compiler_params=_params(("arbitrary",)),
        name="modulation",
    )(c_all, w_mod, b_mod)


def _window_sum(ubuf, stages, c0, pg, w, tm):
    n_stage = w.bit_length() - 1
    read = lambda lo, n: ubuf[lo:lo + n, c0:c0 + pg]
    for k in range(n_stage):
        s = 1 << k
        if k == n_stage - 1:
            return read(HIST_PAD, tm) + read(HIST_PAD - s, tm)
        lo = SUBLANES * (k + 1)
        n = HIST_PAD + tm - lo
        dst = stages[k % 2]
        dst[lo:lo + n, :] = read(lo, n) + read(lo - s, n)
        read = lambda lo, n, src=dst: src[lo:lo + n, :]


def _pool_prompt(u, ubuf, stages, s_idx, tm, wpool_ref, bpool_ref, pscale_ref, pool_o_ref):
    pw = u.shape[1]
    pg = pw // len(POOL_WINDOWS)
    ubuf[HIST_PAD:HIST_PAD + tm, :] = u
    pos = s_idx * tm + lax.broadcasted_iota(jnp.int32, (tm, 1), 0)
    for g, w in enumerate(POOL_WINDOWS):
        c0 = g * pg
        ug = u[:, c0:c0 + pg]
        acc = _window_sum(ubuf, stages, c0, pg, w, tm)
        cnt = jnp.minimum(pos + 1, w).astype(F32)
        d = acc / cnt - ug
        o = jnp.dot(d.astype(BF16), wpool_ref[g], preferred_element_type=F32) + bpool_ref[:, c0:c0 + pg]
        pool_o_ref[:, c0:c0 + pg] = (o * pscale_ref[:, c0:c0 + pg]).astype(BF16)
    ubuf[0:HIST_PAD, :] = ubuf[tm:tm + HIST_PAD, :]


def _pre_kernel(*refs, sample, tm):
    (x_ref, sh1_ref, sc1_ref, norm1_ref, win_ref, qan_ref, wq_ref, kvan_ref, cos_ref, sin_ref, qg_ref, kg_ref,
     *rest) = refs
    if sample:
        wkt_ref, qabs_ref, qr_ref, lat_ref, kr_ref, u_ref = rest
    else:
        (wkv_ref, wpool_ref, bpool_ref, pscale_ref,
         q_ref, k_ref, v_ref, lat_ref, kr_ref, pool_o_ref, utail_ref, ubuf, stage_a, stage_b) = rest
        s_idx = pl.program_id(1)

        @pl.when(s_idx == 0)
        def _():
            ubuf[0:HIST_PAD, :] = jnp.zeros((HIST_PAD, ubuf.shape[1]), F32)

    o_kv = Q_LORA
    o_kr = o_kv + KV_LORA
    o_u = o_kr + LANES
    x = x_ref[...]
    h = _rms(x, norm1_ref[...]) * (1.0 + _rows(sc1_ref[...], tm)) + _rows(sh1_ref[...], tm)
    h = h.astype(BF16)
    cosx = cos_ref[...]
    sinx = sin_ref[...]
    qg = qg_ref[...]
    kg = kg_ref[...]

    cq = jnp.dot(h, win_ref[:, 0:o_kv], preferred_element_type=F32)
    q = jnp.dot(_rms(cq, qan_ref[...]).astype(BF16), wq_ref[...], preferred_element_type=F32)
    for hd in range(N_HEADS):
        c0 = hd * HEAD_PAD
        qn = q[:, c0:c0 + QK_NOPE]
        qr = _rope(q[:, c0 + QK_NOPE:c0 + HEAD_PAD], cosx, sinx)
        r = lax.rsqrt(jnp.sum(qn * qn + qr * qr, axis=-1, keepdims=True) * (1.0 / QK_HEAD) + EPS)
        qn = qn * r * qg[:, 0:QK_NOPE]
        qr = qr * r * qg[:, QK_NOPE:]
        if sample:
            qk = (qn * kg[:, 0:QK_NOPE]).astype(BF16)
            qabs = jnp.dot(qk, wkt_ref[hd * QK_NOPE:(hd + 1) * QK_NOPE, :], preferred_element_type=F32)
            qabs_ref[hd] = qabs.astype(BF16)
            qr_ref[hd] = (qr * kg[:, QK_NOPE:]).astype(BF16)
        else:
            q_ref[hd] = jnp.concatenate([qn, qr], axis=-1).astype(BF16)

    zk = jnp.dot(h, win_ref[:, o_kv:o_u], preferred_element_type=F32)
    lat = _rms(zk[:, 0:KV_LORA], kvan_ref[...])
    lat_ref[...] = lat
    kr = _rope(zk[:, KV_LORA:], cosx, sinx)
    kr_ref[...] = kr[:, 0:QK_ROPE]
    if not sample:
        kv = jnp.dot(lat.astype(BF16), wkv_ref[...], preferred_element_type=F32)
        kr_sq = kr * kr
        for hd in range(N_HEADS):
            c0 = hd * HEAD_PAD
            kn = kv[:, c0:c0 + QK_NOPE]
            rk = lax.rsqrt(jnp.sum(kn * kn + kr_sq, axis=-1, keepdims=True) * (1.0 / QK_HEAD) + EPS)
            k_ref[hd] = jnp.concatenate([kn * rk * kg[:, 0:QK_NOPE], kr * rk * kg[:, QK_NOPE:]], axis=-1).astype(BF16)
            v_ref[hd] = kv[:, c0 + QK_NOPE:c0 + HEAD_PAD].astype(BF16)

    u = jnp.dot(h, win_ref[:, o_u:], preferred_element_type=F32)
    if sample:
        u_ref[...] = u
    else:
        _pool_prompt(u, ubuf, (stage_a, stage_b), s_idx, tm, wpool_ref, bpool_ref, pscale_ref, pool_o_ref)

        @pl.when(s_idx == pl.num_programs(1) - 1)
        def _():
            utail_ref[...] = ubuf[0:HIST_PAD, :]


def _const_spec(a):
    nd = a.ndim
    return pl.BlockSpec(a.shape, lambda *_: (0,) * nd)


def _pre_prompt(x, mod_p, cosx, sinx, w):
    b, s, d = x.shape
    tm = PROMPT_TILE
    pw = w["pool_scale"].shape[1]
    consts_a = [w["norm1"], w["w_in"], w["q_a_norm"], w["w_q"], w["kv_a_norm"]]
    consts_b = [w["q_gain"], w["k_gain"], w["w_kv"], w["w_pool"], w["b_pool"], w["pool_scale"]]
    tok = lambda width: pl.BlockSpec((None, tm, width), lambda i, j: (i, j, 0))
    head = lambda width: pl.BlockSpec((None, N_HEADS, tm, width), lambda i, j: (i, 0, j, 0))
    mod = lambda col: pl.BlockSpec((None, 1, d), lambda i, j: (i, 0, col))
    rope = pl.BlockSpec((tm, LANES), lambda i, j: (j, 0))
    return pl.pallas_call(
        functools.partial(_pre_kernel, sample=False, tm=tm),
        out_shape=(
            jax.ShapeDtypeStruct((b, N_HEADS, s, HEAD_PAD), BF16),
            jax.ShapeDtypeStruct((b, N_HEADS, s, HEAD_PAD), BF16),
            jax.ShapeDtypeStruct((b, N_HEADS, s, V_HEAD), BF16),
            jax.ShapeDtypeStruct((b, s, KV_LORA), F32),
            jax.ShapeDtypeStruct((b, s, QK_ROPE), F32),
            jax.ShapeDtypeStruct((b, s, pw), BF16),
            jax.ShapeDtypeStruct((b, HIST_PAD, pw), F32),
        ),
        grid=(b, s // tm),
        in_specs=[tok(d), mod(0), mod(1)] + [_const_spec(a) for a in consts_a] + [rope, rope]
        + [_const_spec(a) for a in consts_b],
        out_specs=(head(HEAD_PAD), head(HEAD_PAD), head(V_HEAD), tok(KV_LORA), tok(QK_ROPE), tok(pw),
                   pl.BlockSpec((None, HIST_PAD, pw), lambda i, j: (i, 0, 0))),
        scratch_shapes=[pltpu.VMEM((tm + HIST_PAD, pw), F32)]
        + [pltpu.VMEM((tm + HIST_PAD, pw // len(POOL_WINDOWS)), F32)] * 2,
        compiler_params=_params(("arbitrary", "arbitrary")),
        name="pre_prompt",
    )(x, mod_p, mod_p, *consts_a, cosx, sinx, *consts_b)


def _pre_sample(x, mod_s, cosx, sinx, w, bd):
    n, d = x.shape
    pw = w["pool_scale"].shape[1]
    consts = [w["norm1"], w["w_in"], w["q_a_norm"], w["w_q"], w["kv_a_norm"]]
    mod = lambda col: pl.BlockSpec((bd, d), lambda i: (0, col))
    full = lambda *shape: pl.BlockSpec(shape, lambda i: (0,) * len(shape))
    return pl.pallas_call(
        functools.partial(_pre_kernel, sample=True, tm=n),
        out_shape=(
            jax.ShapeDtypeStruct((N_HEADS, n, KV_LORA), BF16),
            jax.ShapeDtypeStruct((N_HEADS, n, LANES), BF16),
            jax.ShapeDtypeStruct((n, KV_LORA), F32),
            jax.ShapeDtypeStruct((n, QK_ROPE), F32),
            jax.ShapeDtypeStruct((n, pw), F32),
        ),
        grid=(1,),
        in_specs=[full(n, d), mod(0), mod(1)] + [_const_spec(a) for a in consts]
        + [full(n, LANES), full(n, LANES), _const_spec(w["q_gain"]), _const_spec(w["k_gain"]), _const_spec(w["w_kt"])],
        out_specs=(full(N_HEADS, n, KV_LORA), full(N_HEADS, n, LANES), full(n, KV_LORA), full(n, QK_ROPE),
                   full(n, pw)),
        compiler_params=_params(("arbitrary",)),
        name="pre_sample",
    )(x, mod_s, mod_s, *consts, cosx, sinx, w["q_gain"], w["k_gain"], w["w_kt"])


def _attn_kernel(q_ref, k_ref, v_ref, o_ref, *, seq, tile):
    n = seq // tile
    row = lax.broadcasted_iota(jnp.int32, (tile, tile), 0)
    col = lax.broadcasted_iota(jnp.int32, (tile, tile), 1)
    for qi in range(n):
        q = q_ref[qi * tile:(qi + 1) * tile, :]
        m = jnp.full((tile, 1), -jnp.inf, F32)
        l = jnp.zeros((tile, 1), F32)
        acc = jnp.zeros((tile, V_HEAD), F32)
        for ki in range(qi + 1):
            k = k_ref[ki * tile:(ki + 1) * tile, :]
            s = lax.dot_general(q, k, _NT, preferred_element_type=F32)
            if ki == qi:
                s = jnp.where(col <= row, s, -jnp.inf)
            m_new = jnp.maximum(m, jnp.max(s, axis=-1, keepdims=True))
            alpha = jnp.exp(m - m_new)
            p = jnp.exp(s - m_new)
            l = alpha * l + jnp.sum(p, axis=-1, keepdims=True)
            acc = alpha * acc + jnp.dot(p.astype(BF16), v_ref[ki * tile:(ki + 1) * tile, :],
                                        preferred_element_type=F32)
            m = m_new
        o_ref[qi * tile:(qi + 1) * tile, :] = (acc / l).astype(BF16)


def _attention_prompt(q, k, v):
    b, h, s, _ = q.shape
    head = lambda width: pl.BlockSpec((None, None, s, width), lambda i, j: (i, j, 0, 0))
    return pl.pallas_call(
        functools.partial(_attn_kernel, seq=s, tile=min(ATTN_TILE, s)),
        out_shape=jax.ShapeDtypeStruct((b, s, h * V_HEAD), BF16),
        grid=(b, h),
        in_specs=[head(HEAD_PAD), head(HEAD_PAD), head(V_HEAD)],
        out_specs=pl.BlockSpec((None, s, V_HEAD), lambda i, j: (i, 0, j)),
        compiler_params=_params(("arbitrary", "arbitrary")),
        name="attention_prompt",
    )(q, k, v)


def _sattn_kernel(pt_ref, q_ref, qr_ref, latn_ref, krn_ref, wkt_ref, clat_hbm, ckr_hbm, ctx_ref,
                  lat_buf, kr_buf, sem, lhs, s_even, s_odd, *, layer, n_seq, n_pages, page, n_new):
    b = pl.program_id(0)
    n_chunks = n_pages // CHUNK_PAGES
    per_stream = (n_seq // SEQ_STREAMS) * n_chunks
    g0 = b * n_chunks
    streams = range(SEQ_STREAMS)
    nq = q_ref.shape[1]
    n_k = wkt_ref.shape[0]
    s_bufs = (s_even, s_odd)

    def slot_of(g):
        return lax.rem(g, CACHE_SLOTS)

    def copies(st, g):
        slot = slot_of(g)
        out = []
        for p in range(CHUNK_PAGES):
            phys = pt_ref[(st * per_stream + g) * CHUNK_PAGES + p]
            dst = pl.ds(p * page, page)
            out.append(pltpu.make_async_copy(clat_hbm.at[layer, phys], lat_buf.at[st, slot, dst, :],
                                             sem.at[st, 0, slot]))
            out.append(pltpu.make_async_copy(ckr_hbm.at[layer, phys], kr_buf.at[st, slot, :, dst],
                                             sem.at[st, 1, slot]))
        return out

    def start(st, g):
        for cp in copies(st, g):
            cp.start()

    def wait(st, g):
        for cp in copies(st, g):
            cp.wait()

    def fetch_ahead(st, g):
        @pl.when(g + PREFETCH_CHUNKS < per_stream)
        def _():
            start(st, g + PREFETCH_CHUNKS)

    @pl.when(b == 0)
    def _():
        for st in streams:
            lhs[st, 0:n_k, :] = wkt_ref[...]
            for g in range(min(PREFETCH_CHUNKS, per_stream)):
                start(st, g)

    for st in streams:
        lhs[st, n_k:n_k + nq, :] = q_ref[st]

    def scores(st, lat, krt):
        t = lat.shape[0]
        big = lax.dot_general(lhs[st], lat.astype(BF16), _NT, preferred_element_type=F32)
        kn = big[0:n_k]
        ssq = jnp.sum((kn * kn).reshape(N_HEADS, QK_NOPE, t), axis=1)
        ssq = ssq + jnp.sum(krt * krt, axis=0, keepdims=True)
        r = lax.rsqrt(ssq * (1.0 / QK_HEAD) + EPS)
        sr = jnp.dot(qr_ref[st, :, 0:QK_ROPE], krt.astype(BF16), preferred_element_type=F32)
        return (big[n_k:] + sr) * jnp.tile(r, (nq // N_HEADS, 1))

    def update(s, lat, carry):
        m, l, acc = carry
        m_new = jnp.maximum(m, jnp.max(s, axis=-1, keepdims=True))
        alpha = jnp.exp(m - m_new)
        p = jnp.exp(s - m_new)
        l = alpha * l + jnp.sum(p, axis=-1, keepdims=True)
        acc = alpha * acc + jnp.dot(p.astype(BF16), lat.astype(BF16), preferred_element_type=F32)
        return m_new, l, acc

    for st in streams:
        wait(st, g0)
        fetch_ahead(st, g0)

    pad = LANES - latn_ref.shape[1]
    key = lax.broadcasted_iota(jnp.int32, (nq, LANES), 1)
    tok = lax.broadcasted_iota(jnp.int32, (nq, LANES), 0) // N_HEADS
    carries = []
    for st in streams:
        lat_new = jnp.concatenate([latn_ref[st], jnp.zeros((pad, KV_LORA), F32)], axis=0)
        s_new = jnp.where((key <= tok) & (key < n_new), scores(st, lat_new, krn_ref[st]), -jnp.inf)
        carry = (jnp.full((nq, 1), -jnp.inf, F32), jnp.zeros((nq, 1), F32), jnp.zeros((nq, KV_LORA), F32))
        carries.append(update(s_new, lat_new, carry))

    for st in streams:
        s_even[st] = scores(st, lat_buf[st, slot_of(g0)], kr_buf[st, slot_of(g0)])

    def enter(c):
        for st in streams:
            wait(st, g0 + c)
            fetch_ahead(st, g0 + c)

    def step(c, parity, carries):
        g = g0 + c
        out = []
        for st in streams:
            s_bufs[parity][st] = scores(st, lat_buf[st, slot_of(g)], kr_buf[st, slot_of(g)])
            out.append(update(s_bufs[1 - parity][st], lat_buf[st, slot_of(g - 1)], carries[st]))
        return tuple(out)

    def two_steps(i, carries):
        c = 1 + 2 * i
        enter(c)
        enter(c + 1)
        return step(c + 1, 0, step(c, 1, carries))

    carries = lax.fori_loop(0, (n_chunks - 1) // 2, two_steps, tuple(carries))
    if n_chunks % 2 == 0:
        enter(n_chunks - 1)
        carries = step(n_chunks - 1, 1, carries)
    for st in streams:
        m, l, acc = update(s_bufs[(n_chunks - 1) % 2][st], lat_buf[st, slot_of(g0 + n_chunks - 1)], carries[st])
        ctx_ref[st] = acc / l


def _attention_sample(page_table, qabs, qr, lat_new, krt_new, w_kt, cache_lat, cache_krt, layer, n_new):
    bd, nq, _ = qabs.shape
    n_pages = page_table.shape[1]
    page = cache_lat.shape[2]
    assert n_pages % CHUNK_PAGES == 0 and n_pages // CHUNK_PAGES >= 2 and bd % SEQ_STREAMS == 0
    rows = CHUNK_PAGES * page
    n_k = w_kt.shape[0]
    pt = page_table.reshape(bd // SEQ_STREAMS, SEQ_STREAMS, n_pages).transpose(1, 0, 2).reshape(-1)
    per_step = lambda r, c: pl.BlockSpec((SEQ_STREAMS, r, c), lambda i, pt: (i, 0, 0))
    grid_spec = pltpu.PrefetchScalarGridSpec(
        num_scalar_prefetch=1,
        grid=(bd // SEQ_STREAMS,),
        in_specs=[
            per_step(nq, KV_LORA), per_step(nq, LANES), per_step(SUBLANES, KV_LORA), per_step(QK_ROPE, LANES),
            pl.BlockSpec((n_k, KV_LORA), lambda i, pt: (0, 0)),
            pl.BlockSpec(memory_space=pl.ANY), pl.BlockSpec(memory_space=pl.ANY),
        ],
        out_specs=per_step(nq, KV_LORA),
        scratch_shapes=[
            pltpu.VMEM((SEQ_STREAMS, CACHE_SLOTS, rows, KV_LORA), F32),
            pltpu.VMEM((SEQ_STREAMS, CACHE_SLOTS, QK_ROPE, rows), F32),
            pltpu.SemaphoreType.DMA((SEQ_STREAMS, 2, CACHE_SLOTS)),
            pltpu.VMEM((SEQ_STREAMS, n_k + nq, KV_LORA), BF16),
            pltpu.VMEM((SEQ_STREAMS, nq, rows), F32),
            pltpu.VMEM((SEQ_STREAMS, nq, rows), F32),
        ],
    )
    return pl.pallas_call(
        functools.partial(_sattn_kernel, layer=layer, n_seq=bd, n_pages=n_pages, page=page, n_new=n_new),
        out_shape=jax.ShapeDtypeStruct((bd, nq, KV_LORA), F32),
        grid_spec=grid_spec,
        compiler_params=_params(("arbitrary",)),
        name="attention_sample",
    )(pt, qabs, qr, lat_new, krt_new, w_kt, cache_lat, cache_krt)


def _ctx_to_heads_kernel(ctx_ref, wv_ref, o_ref):
    n_tok, n_head, bd, _ = ctx_ref.shape
    for t in range(n_tok):
        for hd in range(n_head):
            o = jnp.dot(ctx_ref[t, hd].astype(BF16), wv_ref[hd], preferred_element_type=F32)
            o_ref[t * bd:(t + 1) * bd, hd * V_HEAD:(hd + 1) * V_HEAD] = o.astype(BF16)


def _ctx_to_heads(ctx, w_v):
    n_tok, n_head, bd, _ = ctx.shape
    return pl.pallas_call(
        _ctx_to_heads_kernel,
        out_shape=jax.ShapeDtypeStruct((n_tok * bd, n_head * V_HEAD), BF16),
        grid=(1,),
        in_specs=[_const_spec(ctx), _const_spec(w_v)],
        out_specs=pl.BlockSpec((n_tok * bd, n_head * V_HEAD), lambda i: (0, 0)),
        compiler_params=_params(("arbitrary",)),
        name="ctx_to_heads",
    )(ctx, w_v)


def _pool_sample_kernel(u_ref, hist_ref, wpool_ref, bpool_ref, pscale_ref, o_ref, *, first_pos):
    n_tok, bd, pw = u_ref.shape
    n_hist = hist_ref.shape[0]
    pg = pw // len(POOL_WINDOWS)

    def row(i, c0):
        return hist_ref[i, :, c0:c0 + pg] if i < n_hist else u_ref[i - n_hist, :, c0:c0 + pg]

    for t in range(n_tok):
        for g, w in enumerate(POOL_WINDOWS):
            c0 = g * pg
            ug = u_ref[t, :, c0:c0 + pg]
            acc = ug
            for j in range(1, w):
                acc = acc + row(n_hist + t - j, c0)
            d = acc / float(min(first_pos + t + 1, w)) - ug
            o = jnp.dot(d.astype(BF16), wpool_ref[g], preferred_element_type=F32) + bpool_ref[:, c0:c0 + pg]
            o_ref[t * bd:(t + 1) * bd, c0:c0 + pg] = (o * pscale_ref[:, c0:c0 + pg]).astype(BF16)


def _pool_sample(u, hist, w, first_pos):
    n_tok, bd, pw = u.shape
    args = (u, hist, w["w_pool"], w["b_pool"], w["pool_scale"])
    return pl.pallas_call(
        functools.partial(_pool_sample_kernel, first_pos=first_pos),
        out_shape=jax.ShapeDtypeStruct((n_tok * bd, pw), BF16),
        grid=(1,),
        in_specs=[_const_spec(a) for a in args],
        out_specs=pl.BlockSpec((n_tok * bd, pw), lambda i: (0, 0)),
        compiler_params=_params(("arbitrary",)),
        name="pool_sample",
    )(*args)


def _mix_kernel(x_ref, a_ref, p_ref, wout_ref, g1_ref, sh2_ref, sc2_ref, norm2_ref, x1_ref, h2_ref):
    tm = x_ref.shape[0]
    mixed = jnp.concatenate([a_ref[...], p_ref[...]], axis=1)
    mix = jnp.dot(mixed, wout_ref[...], preferred_element_type=F32)
    x1 = x_ref[...] + _rows(g1_ref[...], tm) * mix
    x1_ref[...] = x1
    h2 = _rms(x1, norm2_ref[...]) * (1.0 + _rows(sc2_ref[...], tm)) + _rows(sh2_ref[...], tm)
    h2_ref[...] = h2.astype(BF16)


def _ffn_kernel(h_ref, x1_ref, g2_ref, wg_hbm, wu_hbm, wd_hbm, y_ref, wg_buf, wu_buf, wd_buf, sem, *, n_steps):
    tf = wd_buf.shape[1]
    n_ff = wd_hbm.shape[0] // tf
    step = pl.program_id(0)
    g0 = step * n_ff
    g2 = _rows(g2_ref[...], y_ref.shape[0])

    def copies(j):
        slot = lax.rem(g0 + j, FFN_WEIGHT_SLOTS)
        cols = pl.ds((j % n_ff) * tf, tf)
        return (pltpu.make_async_copy(wg_hbm.at[:, cols], wg_buf.at[slot], sem.at[0, slot]),
                pltpu.make_async_copy(wu_hbm.at[:, cols], wu_buf.at[slot], sem.at[1, slot]),
                pltpu.make_async_copy(wd_hbm.at[cols, :], wd_buf.at[slot], sem.at[2, slot]))

    def start(j):
        for cp in copies(j):
            cp.start()

    def wait(j):
        for cp in copies(j):
            cp.wait()

    def if_more_steps(fn, j):
        if j < n_ff:
            fn(j)
        else:
            @pl.when(step + 1 < n_steps)
            def _():
                fn(j)

    @pl.when(step == 0)
    def _():
        for j in range(FFN_WEIGHT_SLOTS - 1):
            start(j)
        wait(0)

    y_ref[...] = x1_ref[...]
    h = h_ref[...]
    for j in range(n_ff):
        slot = lax.rem(g0 + j, FFN_WEIGHT_SLOTS)
        gate = jnp.dot(h, wg_buf[slot], preferred_element_type=F32)
        up = jnp.dot(h, wu_buf[slot], preferred_element_type=F32)
        act = (gate * jax.nn.sigmoid(gate) * up).astype(BF16)
        if_more_steps(wait, j + 1)
        if_more_steps(start, j + FFN_WEIGHT_SLOTS - 1)
        part = jnp.dot(act, wd_buf[slot], preferred_element_type=F32)
        y_ref[...] += g2 * part


def _layer_out_prompt(x, attn_o, pool_o, mod_p, w):
    b, s, d = x.shape
    tm = PROMPT_TILE
    tok = lambda width: pl.BlockSpec((None, tm, width), lambda i, j: (i, j, 0))
    mod = lambda col: pl.BlockSpec((None, 1, d), lambda i, j: (i, 0, col))
    x1, h2 = pl.pallas_call(
        _mix_kernel,
        out_shape=(jax.ShapeDtypeStruct((b, s, d), F32), jax.ShapeDtypeStruct((b, s, d), BF16)),
        grid=(b, s // tm),
        in_specs=[tok(d), tok(attn_o.shape[2]), tok(pool_o.shape[2]), _const_spec(w["w_out"]),
                  mod(2), mod(3), mod(4), _const_spec(w["norm2"])],
        out_specs=(tok(d), tok(d)),
        compiler_params=_params(("arbitrary", "arbitrary")),
        name="mix_prompt",
    )(x, attn_o, pool_o, w["w_out"], mod_p, mod_p, mod_p, w["norm2"])

    tm = FFN_TOKEN_TILE
    nst = s // tm
    tok = pl.BlockSpec((None, tm, d), lambda i: (i // nst, i % nst, 0))
    return _ffn(h2, x1, mod_p, w, grid_steps=b * nst, tok_spec=tok,
                g2_spec=pl.BlockSpec((None, 1, d), lambda i: (i // nst, 0, 5)), name="ffn_prompt")


def _ffn(h2, x1, mod, w, *, grid_steps, tok_spec, g2_spec, name):
    d = h2.shape[-1]
    tf = FFN_HIDDEN_TILE
    assert w["w_gate"].shape[1] % tf == 0
    hbm = pl.BlockSpec(memory_space=pl.ANY)
    return pl.pallas_call(
        functools.partial(_ffn_kernel, n_steps=grid_steps),
        out_shape=jax.ShapeDtypeStruct(x1.shape, F32),
        grid=(grid_steps,),
        in_specs=[tok_spec, tok_spec, g2_spec, hbm, hbm, hbm],
        out_specs=tok_spec,
        scratch_shapes=[pltpu.VMEM((FFN_WEIGHT_SLOTS, d, tf), BF16), pltpu.VMEM((FFN_WEIGHT_SLOTS, d, tf), BF16),
                        pltpu.VMEM((FFN_WEIGHT_SLOTS, tf, d), BF16),
                        pltpu.SemaphoreType.DMA((3, FFN_WEIGHT_SLOTS))],
        compiler_params=_params(("arbitrary",)),
        name=name,
    )(h2, x1, mod, w["w_gate"], w["w_up"], w["w_down"])


def _layer_out_sample(x, attn_o, pool_o, mod_s, w, bd):
    n, d = x.shape
    full = lambda *shape: pl.BlockSpec(shape, lambda *_: (0,) * len(shape))
    mod = lambda col: pl.BlockSpec((bd, d), lambda *_: (0, col))
    x1, h2 = pl.pallas_call(
        _mix_kernel,
        out_shape=(jax.ShapeDtypeStruct((n, d), F32), jax.ShapeDtypeStruct((n, d), BF16)),
        grid=(1,),
        in_specs=[full(n, d), full(*attn_o.shape), full(*pool_o.shape), _const_spec(w["w_out"]),
                  mod(2), mod(3), mod(4), _const_spec(w["norm2"])],
        out_specs=(full(n, d), full(n, d)),
        compiler_params=_params(("arbitrary",)),
        name="mix_sample",
    )(x, attn_o, pool_o, w["w_out"], mod_s, mod_s, mod_s, w["norm2"])

    return _ffn(h2, x1, mod_s, w, grid_steps=1, tok_spec=full(n, d), g2_spec=mod(5), name="ffn_sample")


def _rope_tables(pos):
    inv = 1.0 / (ROPE_THETA ** (jnp.arange(0, QK_ROPE, 2, dtype=F32) / QK_ROPE))
    ang = pos.astype(F32)[:, None] * inv[None, :]
    c, s = jnp.cos(ang), jnp.sin(ang)
    z = jnp.zeros((pos.shape[0], LANES - QK_ROPE), F32)
    return jnp.concatenate([c, c, z], axis=1), jnp.concatenate([-s, s, z], axis=1)


def _layer_weights(l, w_in, q_a_norm, w_q_b, kv_a_norm, w_kv_b, q_norm_nope, q_norm_rope, k_norm_nope, k_norm_rope,
                   w_pool, b_pool, pool_scale, w_out, norm1, norm2, w_gate, w_up, w_down):
    half = QK_ROPE // 2
    o_kr = Q_LORA + KV_LORA
    o_u = o_kr + QK_ROPE
    win = w_in[l]
    w_in_p = jnp.concatenate([win[:, :o_u], win[:, o_kr + half:o_u], win[:, o_kr:o_kr + half], win[:, o_u:]], axis=1)
    wq = w_q_b[l].reshape(Q_LORA, N_HEADS, QK_HEAD)
    w_q_p = jnp.concatenate([wq, wq[:, :, QK_NOPE + half:], wq[:, :, QK_NOPE:QK_NOPE + half]], axis=2)
    wkv = w_kv_b[l].reshape(KV_LORA, N_HEADS, QK_NOPE + V_HEAD)
    zpad = jnp.zeros((HEAD_PAD - QK_HEAD,), F32)
    row = lambda a: a.reshape(1, -1)
    return {
        "norm1": row(norm1[l]), "norm2": row(norm2[l]),
        "w_in": w_in_p.astype(BF16),
        "q_a_norm": row(q_a_norm[l]), "kv_a_norm": row(kv_a_norm[l]),
        "w_q": w_q_p.reshape(Q_LORA, N_HEADS * HEAD_PAD).astype(BF16),
        "w_kv": w_kv_b[l].astype(BF16),
        "w_kt": wkv[:, :, :QK_NOPE].reshape(KV_LORA, N_HEADS * QK_NOPE).T.astype(BF16),
        "w_v": wkv[:, :, QK_NOPE:].transpose(1, 0, 2).astype(BF16),
        "q_gain": row(jnp.concatenate([q_norm_nope[l], q_norm_rope[l], q_norm_rope[l], zpad]) * SCALE),
        "k_gain": row(jnp.concatenate([k_norm_nope[l], k_norm_rope[l], k_norm_rope[l], zpad])),
        "w_pool": w_pool[l].astype(BF16), "b_pool": row(b_pool[l]), "pool_scale": row(pool_scale[l]),
        "w_out": w_out[l].astype(BF16),
        "w_gate": w_gate[l].astype(BF16), "w_up": w_up[l].astype(BF16), "w_down": w_down[l].astype(BF16),
    }


def kernel(x_prompt, x_sample, cache_kv_latent, cache_k_rope, state_pool, page_table, c_prompt, c_sample,
           w_mod, b_mod, norm1, w_in, q_a_norm, w_q_b, kv_a_norm, w_kv_b,
           q_norm_nope, q_norm_rope, k_norm_nope, k_norm_rope,
           w_pool, b_pool, pool_scale, w_out, norm2, w_gate, w_up, w_down):
    b, s, d = x_prompt.shape
    bd, t, _ = x_sample.shape
    assert s % PROMPT_TILE == 0 and s % FFN_TOKEN_TILE == 0 and PROMPT_TILE >= HIST_PAD
    assert t <= SUBLANES and bd % SUBLANES == 0
    depth = w_mod.shape[0]
    past = page_table.shape[1] * cache_kv_latent.shape[2]
    cos_p, sin_p = _rope_tables(jnp.arange(s))
    cos_s, sin_s = _rope_tables(jnp.repeat(past + jnp.arange(t), bd))
    c_all = jnp.concatenate([c_sample, c_prompt], axis=0)
    cache_krt = jnp.swapaxes(cache_k_rope, 2, 3)

    yp = x_prompt
    ys = x_sample.transpose(1, 0, 2).reshape(t * bd, d)
    outs = [[] for _ in range(6)]
    for l in range(depth):
        w = _layer_weights(l, w_in, q_a_norm, w_q_b, kv_a_norm, w_kv_b, q_norm_nope, q_norm_rope, k_norm_nope,
                           k_norm_rope, w_pool, b_pool, pool_scale, w_out, norm1, norm2, w_gate, w_up, w_down)
        mod = _modulation(c_all, w_mod[l], b_mod[l].reshape(1, -1))
        mod_p = mod[bd:].reshape(b, 1, -1)
        mod_s = mod

        q, k, v, lat_p, kr_p, pool_p, utail = _pre_prompt(yp, mod_p, cos_p, sin_p, w)
        attn_p = _attention_prompt(q, k, v)
        yp = _layer_out_prompt(yp, attn_p, pool_p, mod_p, w)

        qabs, qr, lat_s, kr_s, u_s = _pre_sample(ys, mod_s, cos_s, sin_s, w, bd)
        per_seq = lambda a: a.reshape(N_HEADS, t, bd, -1).transpose(2, 1, 0, 3).reshape(bd, t * N_HEADS, -1)
        lat_s = lat_s.reshape(t, bd, -1).transpose(1, 0, 2)
        kr_s = kr_s.reshape(t, bd, -1).transpose(1, 0, 2)
        lat_new = jnp.pad(lat_s, ((0, 0), (0, SUBLANES - t), (0, 0)))
        krt_new = jnp.pad(kr_s.transpose(0, 2, 1), ((0, 0), (0, 0), (0, LANES - t)))
        ctx = _attention_sample(page_table, per_seq(qabs), per_seq(qr), lat_new, krt_new, w["w_kt"],
                                cache_kv_latent, cache_krt, l, t)
        attn_s = _ctx_to_heads(ctx.reshape(bd, t, N_HEADS, -1).transpose(1, 2, 0, 3), w["w_v"])
        u_s = u_s.reshape(t, bd, -1)
        pool_s = _pool_sample(u_s, state_pool[l].transpose(1, 0, 2), w, past)
        ys = _layer_out_sample(ys, attn_s, pool_s, mod_s, w, bd)

        full_s = jnp.concatenate([state_pool[l], u_s.transpose(1, 0, 2)], axis=1)
        for dst, val in zip(outs, (lat_p, kr_p, utail[:, HIST_PAD - POOL_HIST:], lat_s, kr_s,
                                   full_s[:, -POOL_HIST:])):
            dst.append(val)

    ys = ys.reshape(t, bd, d).transpose(1, 0, 2)
    return (yp, ys) + tuple(jnp.stack(o) for o in outs)
```

```python
import functools

import jax
import jax.numpy as jnp
from jax import lax
from jax.experimental import pallas as pl
from jax.experimental.pallas import tpu as pltpu

F32 = jnp.float32
BF16 = jnp.bfloat16

N_HEADS = 8
QK_NOPE = 128
QK_ROPE = 64
QK_HEAD = QK_NOPE + QK_ROPE
V_HEAD = 128
Q_LORA = 512
KV_LORA = 256
POOL_WINDOWS = (2, 4, 8, 16)
POOL_HIST = max(POOL_WINDOWS) - 1
ROPE_THETA = 10000.0
EPS = 1e-6
SCALE = QK_HEAD ** -0.5

LANES = 128
SUBLANES = 8
HEAD_PAD = 2 * LANES
assert all(w & (w - 1) == 0 for w in POOL_WINDOWS)
HIST_PAD = SUBLANES * (max(POOL_WINDOWS).bit_length() - 1)
VMEM_LIMIT_BYTES = 56 * 1024 * 1024

MOD_COL_TILE = 1024
PROMPT_TILE = 512
ATTN_TILE = 512
FFN_TOKEN_TILE = 512
FFN_HIDDEN_TILE = 512
FFN_WEIGHT_SLOTS = 4
CHUNK_PAGES = 16
PREFETCH_CHUNKS = 3
CACHE_SLOTS = PREFETCH_CHUNKS + 3
SEQ_STREAMS = 2

_NT = (((1,), (1,)), ((), ()))


def _rms(x, w):
    return x * lax.rsqrt(jnp.mean(x * x, axis=-1, keepdims=True) + EPS) * w


def _rows(a, n):
    r = a.shape[0]
    if r == 1 or r == n:
        return a
    return jnp.tile(a, (n // r, 1))


def _rope(blk, cosx, sinx):
    return blk * cosx + pltpu.roll(blk, LANES // 2, axis=1) * sinx


def _params(semantics):
    return pltpu.CompilerParams(dimension_semantics=semantics, vmem_limit_bytes=VMEM_LIMIT_BYTES)


def _mod_kernel(c_ref, w_ref, b_ref, o_ref):
    c = c_ref[...]
    a = (c * jax.nn.sigmoid(c)).astype(BF16)
    o_ref[...] = jnp.dot(a, w_ref[...].astype(BF16), preferred_element_type=F32) + b_ref[...]


def _modulation(c_all, w_mod, b_mod):
    n, d = c_all.shape
    width = w_mod.shape[1]
    tn = MOD_COL_TILE
    assert width % tn == 0
    return pl.pallas_call(
        _mod_kernel,
        out_shape=jax.ShapeDtypeStruct((n, width), F32),
        grid=(width // tn,),
        in_specs=[
            pl.BlockSpec((n, d), lambda j: (0, 0)),
            pl.BlockSpec((d, tn), lambda j: (0, j)),
            pl.BlockSpec((1, tn), lambda j: (0, j)),
        ],
        out_specs=pl.BlockSpec((n, tn), lambda j: (0, j)),
        compiler_params=_params(("arbitrary",)),
        name="modulation",
    )(c_all, w_mod, b_mod)


def _window_sum(ubuf, stages, c0, pg, w, tm):
    n_stage = w.bit_length() - 1
    read = lambda lo, n: ubuf[lo:lo + n, c0:c0 + pg]
    for k in range(n_stage):
        s = 1 << k
        if k == n_stage - 1:
            return read(HIST_PAD, tm) + read(HIST_PAD - s, tm)
        lo = SUBLANES * (k + 1)
        n = HIST_PAD + tm - lo
        dst = stages[k % 2]
        dst[lo:lo + n, :] = read(lo, n) + read(lo - s, n)
        read = lambda lo, n, src=dst: src[lo:lo + n, :]


def _pool_prompt(u, ubuf, stages, s_idx, tm, wpool_ref, bpool_ref, pscale_ref, pool_o_ref):
    pw = u.shape[1]
    pg = pw // len(POOL_WINDOWS)
    ubuf[HIST_PAD:HIST_PAD + tm, :] = u
    pos = s_idx * tm + lax.broadcasted_iota(jnp.int32, (tm, 1), 0)
    for g, w in enumerate(POOL_WINDOWS):
        c0 = g * pg
        ug = u[:, c0:c0 + pg]
        acc = _window_sum(ubuf, stages, c0, pg, w, tm)
        cnt = jnp.minimum(pos + 1, w).astype(F32)
        d = acc / cnt - ug
        o = jnp.dot(d.astype(BF16), wpool_ref[g], preferred_element_type=F32) + bpool_ref[:, c0:c0 + pg]
        pool_o_ref[:, c0:c0 + pg] = (o * pscale_ref[:, c0:c0 + pg]).astype(BF16)
    ubuf[0:HIST_PAD, :] = ubuf[tm:tm + HIST_PAD, :]


def _pre_kernel(*refs, sample, tm):
    (x_ref, sh1_ref, sc1_ref, norm1_ref, win_ref, qan_ref, wq_ref, kvan_ref, cos_ref, sin_ref, qg_ref, kg_ref,
     *rest) = refs
    if sample:
        wkt_ref, qabs_ref, qr_ref, lat_ref, kr_ref, u_ref = rest
    else:
        (wkv_ref, wpool_ref, bpool_ref, pscale_ref,
         q_ref, k_ref, v_ref, lat_ref, kr_ref, pool_o_ref, utail_ref, ubuf, stage_a, stage_b) = rest
        s_idx = pl.program_id(1)

        @pl.when(s_idx == 0)
        def _():
            ubuf[0:HIST_PAD, :] = jnp.zeros((HIST_PAD, ubuf.shape[1]), F32)

    o_kv = Q_LORA
    o_kr = o_kv + KV_LORA
    o_u = o_kr + LANES
    x = x_ref[...]
    h = _rms(x, norm1_ref[...]) * (1.0 + _rows(sc1_ref[...], tm)) + _rows(sh1_ref[...], tm)
    h = h.astype(BF16)
    cosx = cos_ref[...]
    sinx = sin_ref[...]
    qg = qg_ref[...]
    kg = kg_ref[...]

    cq = jnp.dot(h, win_ref[:, 0:o_kv], preferred_element_type=F32)
    q = jnp.dot(_rms(cq, qan_ref[...]).astype(BF16), wq_ref[...], preferred_element_type=F32)
    for hd in range(N_HEADS):
        c0 = hd * HEAD_PAD
        qn = q[:, c0:c0 + QK_NOPE]
        qr = _rope(q[:, c0 + QK_NOPE:c0 + HEAD_PAD], cosx, sinx)
        r = lax.rsqrt(jnp.sum(qn * qn + qr * qr, axis=-1, keepdims=True) * (1.0 / QK_HEAD) + EPS)
        qn = qn * r * qg[:, 0:QK_NOPE]
        qr = qr * r * qg[:, QK_NOPE:]
        if sample:
            qk = (qn * kg[:, 0:QK_NOPE]).astype(BF16)
            qabs = jnp.dot(qk, wkt_ref[hd * QK_NOPE:(hd + 1) * QK_NOPE, :], preferred_element_type=F32)
            qabs_ref[hd] = qabs.astype(BF16)
            qr_ref[hd] = (qr * kg[:, QK_NOPE:]).astype(BF16)
        else:
            q_ref[hd] = jnp.concatenate([qn, qr], axis=-1).astype(BF16)

    zk = jnp.dot(h, win_ref[:, o_kv:o_u], preferred_element_type=F32)
    lat = _rms(zk[:, 0:KV_LORA], kvan_ref[...])
    lat_ref[...] = lat
    kr = _rope(zk[:, KV_LORA:], cosx, sinx)
    kr_ref[...] = kr[:, 0:QK_ROPE]
    if not sample:
        kv = jnp.dot(lat.astype(BF16), wkv_ref[...], preferred_element_type=F32)
        kr_sq = kr * kr
        for hd in range(N_HEADS):
            c0 = hd * HEAD_PAD
            kn = kv[:, c0:c0 + QK_NOPE]
            rk = lax.rsqrt(jnp.sum(kn * kn + kr_sq, axis=-1, keepdims=True) * (1.0 / QK_HEAD) + EPS)
            k_ref[hd] = jnp.concatenate([kn * rk * kg[:, 0:QK_NOPE], kr * rk * kg[:, QK_NOPE:]], axis=-1).astype(BF16)
            v_ref[hd] = kv[:, c0 + QK_NOPE:c0 + HEAD_PAD].astype(BF16)

    u = jnp.dot(h, win_ref[:, o_u:], preferred_element_type=F32)
    if sample:
        u_ref[...] = u
    else:
        _pool_prompt(u, ubuf, (stage_a, stage_b), s_idx, tm, wpool_ref, bpool_ref, pscale_ref, pool_o_ref)

        @pl.when(s_idx == pl.num_programs(1) - 1)
        def _():
            utail_ref[...] = ubuf[0:HIST_PAD, :]


def _const_spec(a):
    nd = a.ndim
    return pl.BlockSpec(a.shape, lambda *_: (0,) * nd)


def _pre_prompt(x, mod_p, cosx, sinx, w):
    b, s, d = x.shape
    tm = PROMPT_TILE
    pw = w["pool_scale"].shape[1]
    consts_a = [w["norm1"], w["w_in"], w["q_a_norm"], w["w_q"], w["kv_a_norm"]]
    consts_b = [w["q_gain"], w["k_gain"], w["w_kv"], w["w_pool"], w["b_pool"], w["pool_scale"]]
    tok = lambda width: pl.BlockSpec((None, tm, width), lambda i, j: (i, j, 0))
    head = lambda width: pl.BlockSpec((None, N_HEADS, tm, width), lambda i, j: (i, 0, j, 0))
    mod = lambda col: pl.BlockSpec((None, 1, d), lambda i, j: (i, 0, col))
    rope = pl.BlockSpec((tm, LANES), lambda i, j: (j, 0))
    return pl.pallas_call(
        functools.partial(_pre_kernel, sample=False, tm=tm),
        out_shape=(
            jax.ShapeDtypeStruct((b, N_HEADS, s, HEAD_PAD), BF16),
            jax.ShapeDtypeStruct((b, N_HEADS, s, HEAD_PAD), BF16),
            jax.ShapeDtypeStruct((b, N_HEADS, s, V_HEAD), BF16),
            jax.ShapeDtypeStruct((b, s, KV_LORA), F32),
            jax.ShapeDtypeStruct((b, s, QK_ROPE), F32),
            jax.ShapeDtypeStruct((b, s, pw), BF16),
            jax.ShapeDtypeStruct((b, HIST_PAD, pw), F32),
        ),
        grid=(b, s // tm),
        in_specs=[tok(d), mod(0), mod(1)] + [_const_spec(a) for a in consts_a] + [rope, rope]
        + [_const_spec(a) for a in consts_b],
        out_specs=(head(HEAD_PAD), head(HEAD_PAD), head(V_HEAD), tok(KV_LORA), tok(QK_ROPE), tok(pw),
                   pl.BlockSpec((None, HIST_PAD, pw), lambda i, j: (i, 0, 0))),
        scratch_shapes=[pltpu.VMEM((tm + HIST_PAD, pw), F32)]
        + [pltpu.VMEM((tm + HIST_PAD, pw // len(POOL_WINDOWS)), F32)] * 2,
        compiler_params=_params(("arbitrary", "arbitrary")),
        name="pre_prompt",
    )(x, mod_p, mod_p, *consts_a, cosx, sinx, *consts_b)


def _pre_sample(x, mod_s, cosx, sinx, w, bd):
    n, d = x.shape
    pw = w["pool_scale"].shape[1]
    consts = [w["norm1"], w["w_in"], w["q_a_norm"], w["w_q"], w["kv_a_norm"]]
    mod = lambda col: pl.BlockSpec((bd, d), lambda i: (0, col))
    full = lambda *shape: pl.BlockSpec(shape, lambda i: (0,) * len(shape))
    return pl.pallas_call(
        functools.partial(_pre_kernel, sample=True, tm=n),
        out_shape=(
            jax.ShapeDtypeStruct((N_HEADS, n, KV_LORA), BF16),
            jax.ShapeDtypeStruct((N_HEADS, n, LANES), BF16),
            jax.ShapeDtypeStruct((n, KV_LORA), F32),
            jax.ShapeDtypeStruct((n, QK_ROPE), F32),
            jax.ShapeDtypeStruct((n, pw), F32),
        ),
        grid=(1,),
        in_specs=[full(n, d), mod(0), mod(1)] + [_const_spec(a) for a in consts]
        + [full(n, LANES), full(n, LANES), _const_spec(w["q_gain"]), _const_spec(w["k_gain"]), _const_spec(w["w_kt"])],
        out_specs=(full(N_HEADS, n, KV_LORA), full(N_HEADS, n, LANES), full(n, KV_LORA), full(n, QK_ROPE),
                   full(n, pw)),
        compiler_params=_params(("arbitrary",)),
        name="pre_sample",
    )(x, mod_s, mod_s, *consts, cosx, sinx, w["q_gain"], w["k_gain"], w["w_kt"])


def _attn_kernel(q_ref, k_ref, v_ref, o_ref, *, seq, tile):
    n = seq // tile
    row = lax.broadcasted_iota(jnp.int32, (tile, tile), 0)
    col = lax.broadcasted_iota(jnp.int32, (tile, tile), 1)
    for qi in range(n):
        q = q_ref[qi * tile:(qi + 1) * tile, :]
        m = jnp.full((tile, 1), -jnp.inf, F32)
        l = jnp.zeros((tile, 1), F32)
        acc = jnp.zeros((tile, V_HEAD), F32)
        for ki in range(qi + 1):
            k = k_ref[ki * tile:(ki + 1) * tile, :]
            s = lax.dot_general(q, k, _NT, preferred_element_type=F32)
            if ki == qi:
                s = jnp.where(col <= row, s, -jnp.inf)
            m_new = jnp.maximum(m, jnp.max(s, axis=-1, keepdims=True))
            alpha = jnp.exp(m - m_new)
            p = jnp.exp(s - m_new)
            l = alpha * l + jnp.sum(p, axis=-1, keepdims=True)
            acc = alpha * acc + jnp.dot(p.astype(BF16), v_ref[ki * tile:(ki + 1) * tile, :],
                                        preferred_element_type=F32)
            m = m_new
        o_ref[qi * tile:(qi + 1) * tile, :] = (acc / l).astype(BF16)


def _attention_prompt(q, k, v):
    b, h, s, _ = q.shape
    head = lambda width: pl.BlockSpec((None, None, s, width), lambda i, j: (i, j, 0, 0))
    return pl.pallas_call(
        functools.partial(_attn_kernel, seq=s, tile=min(ATTN_TILE, s)),
        out_shape=jax.ShapeDtypeStruct((b, s, h * V_HEAD), BF16),
        grid=(b, h),
        in_specs=[head(HEAD_PAD), head(HEAD_PAD), head(V_HEAD)],
        out_specs=pl.BlockSpec((None, s, V_HEAD), lambda i, j: (i, 0, j)),
        compiler_params=_params(("arbitrary", "arbitrary")),
        name="attention_prompt",
    )(q, k, v)


def _sattn_kernel(pt_ref, q_ref, qr_ref, latn_ref, krn_ref, wkt_ref, clat_hbm, ckr_hbm, ctx_ref,
                  lat_buf, kr_buf, sem, lhs, s_even, s_odd, *, layer, n_seq, n_pages, page, n_new):
    b = pl.program_id(0)
    n_chunks = n_pages // CHUNK_PAGES
    per_stream = (n_seq // SEQ_STREAMS) * n_chunks
    g0 = b * n_chunks
    streams = range(SEQ_STREAMS)
    nq = q_ref.shape[1]
    n_k = wkt_ref.shape[0]
    s_bufs = (s_even, s_odd)

    def slot_of(g):
        return lax.rem(g, CACHE_SLOTS)

    def copies(st, g):
        slot = slot_of(g)
        out = []
        for p in range(CHUNK_PAGES):
            phys = pt_ref[(st * per_stream + g) * CHUNK_PAGES + p]
            dst = pl.ds(p * page, page)
            out.append(pltpu.make_async_copy(clat_hbm.at[layer, phys], lat_buf.at[st, slot, dst, :],
                                             sem.at[st, 0, slot]))
            out.append(pltpu.make_async_copy(ckr_hbm.at[layer, phys], kr_buf.at[st, slot, :, dst],
                                             sem.at[st, 1, slot]))
        return out

    def start(st, g):
        for cp in copies(st, g):
            cp.start()

    def wait(st, g):
        for cp in copies(st, g):
            cp.wait()

    def fetch_ahead(st, g):
        @pl.when(g + PREFETCH_CHUNKS < per_stream)
        def _():
            start(st, g + PREFETCH_CHUNKS)

    @pl.when(b == 0)
    def _():
        for st in streams:
            lhs[st, 0:n_k, :] = wkt_ref[...]
            for g in range(min(PREFETCH_CHUNKS, per_stream)):
                start(st, g)

    for st in streams:
        lhs[st, n_k:n_k + nq, :] = q_ref[st]

    def scores(st, lat, krt):
        t = lat.shape[0]
        big = lax.dot_general(lhs[st], lat.astype(BF16), _NT, preferred_element_type=F32)
        kn = big[0:n_k]
        ssq = jnp.sum((kn * kn).reshape(N_HEADS, QK_NOPE, t), axis=1)
        ssq = ssq + jnp.sum(krt * krt, axis=0, keepdims=True)
        r = lax.rsqrt(ssq * (1.0 / QK_HEAD) + EPS)
        sr = jnp.dot(qr_ref[st, :, 0:QK_ROPE], krt.astype(BF16), preferred_element_type=F32)
        return (big[n_k:] + sr) * jnp.tile(r, (nq // N_HEADS, 1))

    def update(s, lat, carry):
        m, l, acc = carry
        m_new = jnp.maximum(m, jnp.max(s, axis=-1, keepdims=True))
        alpha = jnp.exp(m - m_new)
        p = jnp.exp(s - m_new)
        l = alpha * l + jnp.sum(p, axis=-1, keepdims=True)
        acc = alpha * acc + jnp.dot(p.astype(BF16), lat.astype(BF16), preferred_element_type=F32)
        return m_new, l, acc

    for st in streams:
        wait(st, g0)
        fetch_ahead(st, g0)

    pad = LANES - latn_ref.shape[1]
    key = lax.broadcasted_iota(jnp.int32, (nq, LANES), 1)
    tok = lax.broadcasted_iota(jnp.int32, (nq, LANES), 0) // N_HEADS
    carries = []
    for st in streams:
        lat_new = jnp.concatenate([latn_ref[st], jnp.zeros((pad, KV_LORA), F32)], axis=0)
        s_new = jnp.where((key <= tok) & (key < n_new), scores(st, lat_new, krn_ref[st]), -jnp.inf)
        carry = (jnp.full((nq, 1), -jnp.inf, F32), jnp.zeros((nq, 1), F32), jnp.zeros((nq, KV_LORA), F32))
        carries.append(update(s_new, lat_new, carry))

    for st in streams:
        s_even[st] = scores(st, lat_buf[st, slot_of(g0)], kr_buf[st, slot_of(g0)])

    def enter(c):
        for st in streams:
            wait(st, g0 + c)
            fetch_ahead(st, g0 + c)

    def step(c, parity, carries):
        g = g0 + c
        out = []
        for st in streams:
            s_bufs[parity][st] = scores(st, lat_buf[st, slot_of(g)], kr_buf[st, slot_of(g)])
            out.append(update(s_bufs[1 - parity][st], lat_buf[st, slot_of(g - 1)], carries[st]))
        return tuple(out)

    def two_steps(i, carries):
        c = 1 + 2 * i
        enter(c)
        enter(c + 1)
        return step(c + 1, 0, step(c, 1, carries))

    carries = lax.fori_loop(0, (n_chunks - 1) // 2, two_steps, tuple(carries))
    if n_chunks % 2 == 0:
        enter(n_chunks - 1)
        carries = step(n_chunks - 1, 1, carries)
    for st in streams:
        m, l, acc = update(s_bufs[(n_chunks - 1) % 2][st], lat_buf[st, slot_of(g0 + n_chunks - 1)], carries[st])
        ctx_ref[st] = acc / l


def _attention_sample(page_table, qabs, qr, lat_new, krt_new, w_kt, cache_lat, cache_krt, layer, n_new):
    bd, nq, _ = qabs.shape
    n_pages = page_table.shape[1]
    page = cache_lat.shape[2]
    assert n_pages % CHUNK_PAGES == 0 and n_pages // CHUNK_PAGES >= 2 and bd % SEQ_STREAMS == 0
    rows = CHUNK_PAGES * page
    n_k = w_kt.shape[0]
    pt = page_table.reshape(bd // SEQ_STREAMS, SEQ_STREAMS, n_pages).transpose(1, 0, 2).reshape(-1)
    per_step = lambda r, c: pl.BlockSpec((SEQ_STREAMS, r, c), lambda i, pt: (i, 0, 0))
    grid_spec = pltpu.PrefetchScalarGridSpec(
        num_scalar_prefetch=1,
        grid=(bd // SEQ_STREAMS,),
        in_specs=[
            per_step(nq, KV_LORA), per_step(nq, LANES), per_step(SUBLANES, KV_LORA), per_step(QK_ROPE, LANES),
            pl.BlockSpec((n_k, KV_LORA), lambda i, pt: (0, 0)),
            pl.BlockSpec(memory_space=pl.ANY), pl.BlockSpec(memory_space=pl.ANY),
        ],
        out_specs=per_step(nq, KV_LORA),
        scratch_shapes=[
            pltpu.VMEM((SEQ_STREAMS, CACHE_SLOTS, rows, KV_LORA), F32),
            pltpu.VMEM((SEQ_STREAMS, CACHE_SLOTS, QK_ROPE, rows), F32),
            pltpu.SemaphoreType.DMA((SEQ_STREAMS, 2, CACHE_SLOTS)),
            pltpu.VMEM((SEQ_STREAMS, n_k + nq, KV_LORA), BF16),
            pltpu.VMEM((SEQ_STREAMS, nq, rows), F32),
            pltpu.VMEM((SEQ_STREAMS, nq, rows), F32),
        ],
    )
    return pl.pallas_call(
        functools.partial(_sattn_kernel, layer=layer, n_seq=bd, n_pages=n_pages, page=page, n_new=n_new),
        out_shape=jax.ShapeDtypeStruct((bd, nq, KV_LORA), F32),
        grid_spec=grid_spec,
        compiler_params=_params(("arbitrary",)),
        name="attention_sample",
    )(pt, qabs, qr, lat_new, krt_new, w_kt, cache_lat, cache_krt)


def _ctx_to_heads_kernel(ctx_ref, wv_ref, o_ref):
    n_tok, n_head, bd, _ = ctx_ref.shape
    for t in range(n_tok):
        for hd in range(n_head):
            o = jnp.dot(ctx_ref[t, hd].astype(BF16), wv_ref[hd], preferred_element_type=F32)
            o_ref[t * bd:(t + 1) * bd, hd * V_HEAD:(hd + 1) * V_HEAD] = o.astype(BF16)


def _ctx_to_heads(ctx, w_v):
    n_tok, n_head, bd, _ = ctx.shape
    return pl.pallas_call(
        _ctx_to_heads_kernel,
        out_shape=jax.ShapeDtypeStruct((n_tok * bd, n_head * V_HEAD), BF16),
        grid=(1,),
        in_specs=[_const_spec(ctx), _const_spec(w_v)],
        out_specs=pl.BlockSpec((n_tok * bd, n_head * V_HEAD), lambda i: (0, 0)),
        compiler_params=_params(("arbitrary",)),
        name="ctx_to_heads",
    )(ctx, w_v)


def _pool_sample_kernel(u_ref, hist_ref, wpool_ref, bpool_ref, pscale_ref, o_ref, *, first_pos):
    n_tok, bd, pw = u_ref.shape
    n_hist = hist_ref.shape[0]
    pg = pw // len(POOL_WINDOWS)

    def row(i, c0):
        return hist_ref[i, :, c0:c0 + pg] if i < n_hist else u_ref[i - n_hist, :, c0:c0 + pg]

    for t in range(n_tok):
        for g, w in enumerate(POOL_WINDOWS):
            c0 = g * pg
            ug = u_ref[t, :, c0:c0 + pg]
            acc = ug
            for j in range(1, w):
                acc = acc + row(n_hist + t - j, c0)
            d = acc / float(min(first_pos + t + 1, w)) - ug
            o = jnp.dot(d.astype(BF16), wpool_ref[g], preferred_element_type=F32) + bpool_ref[:, c0:c0 + pg]
            o_ref[t * bd:(t + 1) * bd, c0:c0 + pg] = (o * pscale_ref[:, c0:c0 + pg]).astype(BF16)


def _pool_sample(u, hist, w, first_pos):
    n_tok, bd, pw = u.shape
    args = (u, hist, w["w_pool"], w["b_pool"], w["pool_scale"])
    return pl.pallas_call(
        functools.partial(_pool_sample_kernel, first_pos=first_pos),
        out_shape=jax.ShapeDtypeStruct((n_tok * bd, pw), BF16),
        grid=(1,),
        in_specs=[_const_spec(a) for a in args],
        out_specs=pl.BlockSpec((n_tok * bd, pw), lambda i: (0, 0)),
        compiler_params=_params(("arbitrary",)),
        name="pool_sample",
    )(*args)


def _mix_kernel(x_ref, a_ref, p_ref, wout_ref, g1_ref, sh2_ref, sc2_ref, norm2_ref, x1_ref, h2_ref):
    tm = x_ref.shape[0]
    mixed = jnp.concatenate([a_ref[...], p_ref[...]], axis=1)
    mix = jnp.dot(mixed, wout_ref[...], preferred_element_type=F32)
    x1 = x_ref[...] + _rows(g1_ref[...], tm) * mix
    x1_ref[...] = x1
    h2 = _rms(x1, norm2_ref[...]) * (1.0 + _rows(sc2_ref[...], tm)) + _rows(sh2_ref[...], tm)
    h2_ref[...] = h2.astype(BF16)


def _ffn_kernel(h_ref, x1_ref, g2_ref, wg_hbm, wu_hbm, wd_hbm, y_ref, wg_buf, wu_buf, wd_buf, sem, *, n_steps):
    tf = wd_buf.shape[1]
    n_ff = wd_hbm.shape[0] // tf
    step = pl.program_id(0)
    g0 = step * n_ff
    g2 = _rows(g2_ref[...], y_ref.shape[0])

    def copies(j):
        slot = lax.rem(g0 + j, FFN_WEIGHT_SLOTS)
        cols = pl.ds((j % n_ff) * tf, tf)
        return (pltpu.make_async_copy(wg_hbm.at[:, cols], wg_buf.at[slot], sem.at[0, slot]),
                pltpu.make_async_copy(wu_hbm.at[:, cols], wu_buf.at[slot], sem.at[1, slot]),
                pltpu.make_async_copy(wd_hbm.at[cols, :], wd_buf.at[slot], sem.at[2, slot]))

    def start(j):
        for cp in copies(j):
            cp.start()

    def wait(j):
        for cp in copies(j):
            cp.wait()

    def if_more_steps(fn, j):
        if j < n_ff:
            fn(j)
        else:
            @pl.when(step + 1 < n_steps)
            def _():
                fn(j)

    @pl.when(step == 0)
    def _():
        for j in range(FFN_WEIGHT_SLOTS - 1):
            start(j)
        wait(0)

    y_ref[...] = x1_ref[...]
    h = h_ref[...]
    for j in range(n_ff):
        slot = lax.rem(g0 + j, FFN_WEIGHT_SLOTS)
        gate = jnp.dot(h, wg_buf[slot], preferred_element_type=F32)
        up = jnp.dot(h, wu_buf[slot], preferred_element_type=F32)
        act = (gate * jax.nn.sigmoid(gate) * up).astype(BF16)
        if_more_steps(wait, j + 1)
        if_more_steps(start, j + FFN_WEIGHT_SLOTS - 1)
        part = jnp.dot(act, wd_buf[slot], preferred_element_type=F32)
        y_ref[...] += g2 * part


def _layer_out_prompt(x, attn_o, pool_o, mod_p, w):
    b, s, d = x.shape
    tm = PROMPT_TILE
    tok = lambda width: pl.BlockSpec((None, tm, width), lambda i, j: (i, j, 0))
    mod = lambda col: pl.BlockSpec((None, 1, d), lambda i, j: (i, 0, col))
    x1, h2 = pl.pallas_call(
        _mix_kernel,
        out_shape=(jax.ShapeDtypeStruct((b, s, d), F32), jax.ShapeDtypeStruct((b, s, d), BF16)),
        grid=(b, s // tm),
        in_specs=[tok(d), tok(attn_o.shape[2]), tok(pool_o.shape[2]), _const_spec(w["w_out"]),
                  mod(2), mod(3), mod(4), _const_spec(w["norm2"])],
        out_specs=(tok(d), tok(d)),
        compiler_params=_params(("arbitrary", "arbitrary")),
        name="mix_prompt",
    )(x, attn_o, pool_o, w["w_out"], mod_p, mod_p, mod_p, w["norm2"])

    tm = FFN_TOKEN_TILE
    nst = s // tm
    tok = pl.BlockSpec((None, tm, d), lambda i: (i // nst, i % nst, 0))
    return _ffn(h2, x1, mod_p, w, grid_steps=b * nst, tok_spec=tok,
                g2_spec=pl.BlockSpec((None, 1, d), lambda i: (i // nst, 0, 5)), name="ffn_prompt")


def _ffn(h2, x1, mod, w, *, grid_steps, tok_spec, g2_spec, name):
    d = h2.shape[-1]
    tf = FFN_HIDDEN_TILE
    assert w["w_gate"].shape[1] % tf == 0
    hbm = pl.BlockSpec(memory_space=pl.ANY)
    return pl.pallas_call(
        functools.partial(_ffn_kernel, n_steps=grid_steps),
        out_shape=jax.ShapeDtypeStruct(x1.shape, F32),
        grid=(grid_steps,),
        in_specs=[tok_spec, tok_spec, g2_spec, hbm, hbm, hbm],
        out_specs=tok_spec,
        scratch_shapes=[pltpu.VMEM((FFN_WEIGHT_SLOTS, d, tf), BF16), pltpu.VMEM((FFN_WEIGHT_SLOTS, d, tf), BF16),
                        pltpu.VMEM((FFN_WEIGHT_SLOTS, tf, d), BF16),
                        pltpu.SemaphoreType.DMA((3, FFN_WEIGHT_SLOTS))],
        compiler_params=_params(("arbitrary",)),
        name=name,
    )(h2, x1, mod, w["w_gate"], w["w_up"], w["w_down"])


def _layer_out_sample(x, attn_o, pool_o, mod_s, w, bd):
    n, d = x.shape
    full = lambda *shape: pl.BlockSpec(shape, lambda *_: (0,) * len(shape))
    mod = lambda col: pl.BlockSpec((bd, d), lambda *_: (0, col))
    x1, h2 = pl.pallas_call(
        _mix_kernel,
        out_shape=(jax.ShapeDtypeStruct((n, d), F32), jax.ShapeDtypeStruct((n, d), BF16)),
        grid=(1,),
        in_specs=[full(n, d), full(*attn_o.shape), full(*pool_o.shape), _const_spec(w["w_out"]),
                  mod(2), mod(3), mod(4), _const_spec(w["norm2"])],
        out_specs=(full(n, d), full(n, d)),
        compiler_params=_params(("arbitrary",)),
        name="mix_sample",
    )(x, attn_o, pool_o, w["w_out"], mod_s, mod_s, mod_s, w["norm2"])

    return _ffn(h2, x1, mod_s, w, grid_steps=1, tok_spec=full(n, d), g2_spec=mod(5), name="ffn_sample")


def _rope_tables(pos):
    inv = 1.0 / (ROPE_THETA ** (jnp.arange(0, QK_ROPE, 2, dtype=F32) / QK_ROPE))
    ang = pos.astype(F32)[:, None] * inv[None, :]
    c, s = jnp.cos(ang), jnp.sin(ang)
    z = jnp.zeros((pos.shape[0], LANES - QK_ROPE), F32)
    return jnp.concatenate([c, c, z], axis=1), jnp.concatenate([-s, s, z], axis=1)


def _layer_weights(l, w_in, q_a_norm, w_q_b, kv_a_norm, w_kv_b, q_norm_nope, q_norm_rope, k_norm_nope, k_norm_rope,
                   w_pool, b_pool, pool_scale, w_out, norm1, norm2, w_gate, w_up, w_down):
    half = QK_ROPE // 2
    o_kr = Q_LORA + KV_LORA
    o_u = o_kr + QK_ROPE
    win = w_in[l]
    w_in_p = jnp.concatenate([win[:, :o_u], win[:, o_kr + half:o_u], win[:, o_kr:o_kr + half], win[:, o_u:]], axis=1)
    wq = w_q_b[l].reshape(Q_LORA, N_HEADS, QK_HEAD)
    w_q_p = jnp.concatenate([wq, wq[:, :, QK_NOPE + half:], wq[:, :, QK_NOPE:QK_NOPE + half]], axis=2)
    wkv = w_kv_b[l].reshape(KV_LORA, N_HEADS, QK_NOPE + V_HEAD)
    zpad = jnp.zeros((HEAD_PAD - QK_HEAD,), F32)
    row = lambda a: a.reshape(1, -1)
    return {
        "norm1": row(norm1[l]), "norm2": row(norm2[l]),
        "w_in": w_in_p.astype(BF16),
        "q_a_norm": row(q_a_norm[l]), "kv_a_norm": row(kv_a_norm[l]),
        "w_q": w_q_p.reshape(Q_LORA, N_HEADS * HEAD_PAD).astype(BF16),
        "w_kv": w_kv_b[l].astype(BF16),
        "w_kt": wkv[:, :, :QK_NOPE].reshape(KV_LORA, N_HEADS * QK_NOPE).T.astype(BF16),
        "w_v": wkv[:, :, QK_NOPE:].transpose(1, 0, 2).astype(BF16),
        "q_gain": row(jnp.concatenate([q_norm_nope[l], q_norm_rope[l], q_norm_rope[l], zpad]) * SCALE),
        "k_gain": row(jnp.concatenate([k_norm_nope[l], k_norm_rope[l], k_norm_rope[l], zpad])),
        "w_pool": w_pool[l].astype(BF16), "b_pool": row(b_pool[l]), "pool_scale": row(pool_scale[l]),
        "w_out": w_out[l].astype(BF16),
        "w_gate": w_gate[l].astype(BF16), "w_up": w_up[l].astype(BF16), "w_down": w_down[l].astype(BF16),
    }


def kernel(x_prompt, x_sample, cache_kv_latent, cache_k_rope, state_pool, page_table, c_prompt, c_sample,
           w_mod, b_mod, norm1, w_in, q_a_norm, w_q_b, kv_a_norm, w_kv_b,
           q_norm_nope, q_norm_rope, k_norm_nope, k_norm_rope,
           w_pool, b_pool, pool_scale, w_out, norm2, w_gate, w_up, w_down):
    b, s, d = x_prompt.shape
    bd, t, _ = x_sample.shape
    assert s % PROMPT_TILE == 0 and s % FFN_TOKEN_TILE == 0 and PROMPT_TILE >= HIST_PAD
    assert t <= SUBLANES and bd % SUBLANES == 0
    depth = w_mod.shape[0]
    past = page_table.shape[1] * cache_kv_latent.shape[2]
    cos_p, sin_p = _rope_tables(jnp.arange(s))
    cos_s, sin_s = _rope_tables(jnp.repeat(past + jnp.arange(t), bd))
    c_all = jnp.concatenate([c_sample, c_prompt], axis=0)
    cache_krt = jnp.swapaxes(cache_k_rope, 2, 3)

    yp = x_prompt
    ys = x_sample.transpose(1, 0, 2).reshape(t * bd, d)
    outs = [[] for _ in range(6)]
    for l in range(depth):
        w = _layer_weights(l, w_in, q_a_norm, w_q_b, kv_a_norm, w_kv_b, q_norm_nope, q_norm_rope, k_norm_nope,
                           k_norm_rope, w_pool, b_pool, pool_scale, w_out, norm1, norm2, w_gate, w_up, w_down)
        mod = _modulation(c_all, w_mod[l], b_mod[l].reshape(1, -1))
        mod_p = mod[bd:].reshape(b, 1, -1)
        mod_s = mod

        q, k, v, lat_p, kr_p, pool_p, utail = _pre_prompt(yp, mod_p, cos_p, sin_p, w)
        attn_p = _attention_prompt(q, k, v)
        yp = _layer_out_prompt(yp, attn_p, pool_p, mod_p, w)

        qabs, qr, lat_s, kr_s, u_s = _pre_sample(ys, mod_s, cos_s, sin_s, w, bd)
        per_seq = lambda a: a.reshape(N_HEADS, t, bd, -1).transpose(2, 1, 0, 3).reshape(bd, t * N_HEADS, -1)
        lat_s = lat_s.reshape(t, bd, -1).transpose(1, 0, 2)
        kr_s = kr_s.reshape(t, bd, -1).transpose(1, 0, 2)
        lat_new = jnp.pad(lat_s, ((0, 0), (0, SUBLANES - t), (0, 0)))
        krt_new = jnp.pad(kr_s.transpose(0, 2, 1), ((0, 0), (0, 0), (0, LANES - t)))
        ctx = _attention_sample(page_table, per_seq(qabs), per_seq(qr), lat_new, krt_new, w["w_kt"],
                                cache_kv_latent, cache_krt, l, t)
        attn_s = _ctx_to_heads(ctx.reshape(bd, t, N_HEADS, -1).transpose(1, 2, 0, 3), w["w_v"])
        u_s = u_s.reshape(t, bd, -1)
        pool_s = _pool_sample(u_s, state_pool[l].transpose(1, 0, 2), w, past)
        ys = _layer_out_sample(ys, attn_s, pool_s, mod_s, w, bd)

        full_s = jnp.concatenate([state_pool[l], u_s.transpose(1, 0, 2)], axis=1)
        for dst, val in zip(outs, (lat_p, kr_p, utail[:, HIST_PAD - POOL_HIST:], lat_s, kr_s,
                                   full_s[:, -POOL_HIST:])):
            dst.append(val)

    ys = ys.reshape(t, bd, d).transpose(1, 0, 2)
    return (yp, ys) + tuple(jnp.stack(o) for o in outs)
```

```python
import functools

import jax
import jax.numpy as jnp
from jax import lax
from jax.experimental import pallas as pl
from jax.experimental.pallas import tpu as pltpu

F32 = jnp.float32
BF16 = jnp.bfloat16

N_HEADS = 8
QK_NOPE = 128
QK_ROPE = 64
QK_HEAD = QK_NOPE + QK_ROPE
V_HEAD = 128
Q_LORA = 512
KV_LORA = 256
POOL_WINDOWS = (2, 4, 8, 16)
POOL_HIST = max(POOL_WINDOWS) - 1
ROPE_THETA = 10000.0
EPS = 1e-6
SCALE = QK_HEAD ** -0.5

LANES = 128
SUBLANES = 8
HEAD_PAD = 2 * LANES
assert all(w & (w - 1) == 0 for w in POOL_WINDOWS)
HIST_PAD = SUBLANES * (max(POOL_WINDOWS).bit_length() - 1)
VMEM_LIMIT_BYTES = 56 * 1024 * 1024

MOD_COL_TILE = 1024
PROMPT_TILE = 512
ATTN_TILE = 512
FFN_TOKEN_TILE = 512
FFN_HIDDEN_TILE = 512
FFN_WEIGHT_SLOTS = 4
CHUNK_PAGES = 16
PREFETCH_CHUNKS = 3
CACHE_SLOTS = PREFETCH_CHUNKS + 3
SEQ_STREAMS = 2

_NT = (((1,), (1,)), ((), ()))


def _rms(x, w):
    return x * lax.rsqrt(jnp.mean(x * x, axis=-1, keepdims=True) + EPS) * w


def _rows(a, n):
    r = a.shape[0]
    if r == 1 or r == n:
        return a
    return jnp.tile(a, (n // r, 1))


def _rope(blk, cosx, sinx):
    return blk * cosx + pltpu.roll(blk, LANES // 2, axis=1) * sinx


def _params(semantics):
    return pltpu.CompilerParams(dimension_semantics=semantics, vmem_limit_bytes=VMEM_LIMIT_BYTES)


def _mod_kernel(c_ref, w_ref, b_ref, o_ref):
    c = c_ref[...]
    a = (c * jax.nn.sigmoid(c)).astype(BF16)
    o_ref[...] = jnp.dot(a, w_ref[...].astype(BF16), preferred_element_type=F32) + b_ref[...]


def _modulation(c_all, w_mod, b_mod):
    n, d = c_all.shape
    width = w_mod.shape[1]
    tn = MOD_COL_TILE
    assert width % tn == 0
    return pl.pallas_call(
        _mod_kernel,
        out_shape=jax.ShapeDtypeStruct((n, width), F32),
        grid=(width // tn,),
        in_specs=[
            pl.BlockSpec((n, d), lambda j: (0, 0)),
            pl.BlockSpec((d, tn), lambda j: (0, j)),
            pl.BlockSpec((1, tn), lambda j: (0, j)),
        ],
        out_specs=pl.BlockSpec((n, tn), lambda j: (0, j)),
        compiler_params=_params(("arbitrary",)),
        name="modulation",
    )(c_all, w_mod, b_mod)


def _window_sum(ubuf, stages, c0, pg, w, tm):
    n_stage = w.bit_length() - 1
    read = lambda lo, n: ubuf[lo:lo + n, c0:c0 + pg]
    for k in range(n_stage):
        s = 1 << k
        if k == n_stage - 1:
            return read(HIST_PAD, tm) + read(HIST_PAD - s, tm)
        lo = SUBLANES * (k + 1)
        n = HIST_PAD + tm - lo
        dst = stages[k % 2]
        dst[lo:lo + n, :] = read(lo, n) + read(lo - s, n)
        read = lambda lo, n, src=dst: src[lo:lo + n, :]


def _pool_prompt(u, ubuf, stages, s_idx, tm, wpool_ref, bpool_ref, pscale_ref, pool_o_ref):
    pw = u.shape[1]
    pg = pw // len(POOL_WINDOWS)
    ubuf[HIST_PAD:HIST_PAD + tm, :] = u
    pos = s_idx * tm + lax.broadcasted_iota(jnp.int32, (tm, 1), 0)
    for g, w in enumerate(POOL_WINDOWS):
        c0 = g * pg
        ug = u[:, c0:c0 + pg]
        acc = _window_sum(ubuf, stages, c0, pg, w, tm)
        cnt = jnp.minimum(pos + 1, w).astype(F32)
        d = acc / cnt - ug
        o = jnp.dot(d.astype(BF16), wpool_ref[g], preferred_element_type=F32) + bpool_ref[:, c0:c0 + pg]
        pool_o_ref[:, c0:c0 + pg] = (o * pscale_ref[:, c0:c0 + pg]).astype(BF16)
    ubuf[0:HIST_PAD, :] = ubuf[tm:tm + HIST_PAD, :]


def _pre_kernel(*refs, sample, tm):
    (x_ref, sh1_ref, sc1_ref, norm1_ref, win_ref, qan_ref, wq_ref, kvan_ref, cos_ref, sin_ref, qg_ref, kg_ref,
     *rest) = refs
    if sample:
        wkt_ref, qabs_ref, qr_ref, lat_ref, kr_ref, u_ref = rest
    else:
        (wkv_ref, wpool_ref, bpool_ref, pscale_ref,
         q_ref, k_ref, v_ref, lat_ref, kr_ref, pool_o_ref, utail_ref, ubuf, stage_a, stage_b) = rest
        s_idx = pl.program_id(1)

        @pl.when(s_idx == 0)
        def _():
            ubuf[0:HIST_PAD, :] = jnp.zeros((HIST_PAD, ubuf.shape[1]), F32)

    o_kv = Q_LORA
    o_kr = o_kv + KV_LORA
    o_u = o_kr + LANES
    x = x_ref[...]
    h = _rms(x, norm1_ref[...]) * (1.0 + _rows(sc1_ref[...], tm)) + _rows(sh1_ref[...], tm)
    h = h.astype(BF16)
    cosx = cos_ref[...]
    sinx = sin_ref[...]
    qg = qg_ref[...]
    kg = kg_ref[...]

    cq = jnp.dot(h, win_ref[:, 0:o_kv], preferred_element_type=F32)
    q = jnp.dot(_rms(cq, qan_ref[...]).astype(BF16), wq_ref[...], preferred_element_type=F32)
    for hd in range(N_HEADS):
        c0 = hd * HEAD_PAD
        qn = q[:, c0:c0 + QK_NOPE]
        qr = _rope(q[:, c0 + QK_NOPE:c0 + HEAD_PAD], cosx, sinx)
        r = lax.rsqrt(jnp.sum(qn * qn + qr * qr, axis=-1, keepdims=True) * (1.0 / QK_HEAD) + EPS)
        qn = qn * r * qg[:, 0:QK_NOPE]
        qr = qr * r * qg[:, QK_NOPE:]
        if sample:
            qk = (qn * kg[:, 0:QK_NOPE]).astype(BF16)
            qabs = jnp.dot(qk, wkt_ref[hd * QK_NOPE:(hd + 1) * QK_NOPE, :], preferred_element_type=F32)
            qabs_ref[hd] = qabs.astype(BF16)
            qr_ref[hd] = (qr * kg[:, QK_NOPE:]).astype(BF16)
        else:
            q_ref[hd] = jnp.concatenate([qn, qr], axis=-1).astype(BF16)

    zk = jnp.dot(h, win_ref[:, o_kv:o_u], preferred_element_type=F32)
    lat = _rms(zk[:, 0:KV_LORA], kvan_ref[...])
    lat_ref[...] = lat
    kr = _rope(zk[:, KV_LORA:], cosx, sinx)
    kr_ref[...] = kr[:, 0:QK_ROPE]
    if not sample:
        kv = jnp.dot(lat.astype(BF16), wkv_ref[...], preferred_element_type=F32)
        kr_sq = kr * kr
        for hd in range(N_HEADS):
            c0 = hd * HEAD_PAD
            kn = kv[:, c0:c0 + QK_NOPE]
            rk = lax.rsqrt(jnp.sum(kn * kn + kr_sq, axis=-1, keepdims=True) * (1.0 / QK_HEAD) + EPS)
            k_ref[hd] = jnp.concatenate([kn * rk * kg[:, 0:QK_NOPE], kr * rk * kg[:, QK_NOPE:]], axis=-1).astype(BF16)
            v_ref[hd] = kv[:, c0 + QK_NOPE:c0 + HEAD_PAD].astype(BF16)

    u = jnp.dot(h, win_ref[:, o_u:], preferred_element_type=F32)
    if sample:
        u_ref[...] = u
    else:
        _pool_prompt(u, ubuf, (stage_a, stage_b), s_idx, tm, wpool_ref, bpool_ref, pscale_ref, pool_o_ref)

        @pl.when(s_idx == pl.num_programs(1) - 1)
        def _():
            utail_ref[...] = ubuf[0:HIST_PAD, :]


def _const_spec(a):
    nd = a.ndim
    return pl.BlockSpec(a.shape, lambda *_: (0,) * nd)


def _pre_prompt(x, mod_p, cosx, sinx, w):
    b, s, d = x.shape
    tm = PROMPT_TILE
    pw = w["pool_scale"].shape[1]
    consts_a = [w["norm1"], w["w_in"], w["q_a_norm"], w["w_q"], w["kv_a_norm"]]
    consts_b = [w["q_gain"], w["k_gain"], w["w_kv"], w["w_pool"], w["b_pool"], w["pool_scale"]]
    tok = lambda width: pl.BlockSpec((None, tm, width), lambda i, j: (i, j, 0))
    head = lambda width: pl.BlockSpec((None, N_HEADS, tm, width), lambda i, j: (i, 0, j, 0))
    mod = lambda col: pl.BlockSpec((None, 1, d), lambda i, j: (i, 0, col))
    rope = pl.BlockSpec((tm, LANES), lambda i, j: (j, 0))
    return pl.pallas_call(
        functools.partial(_pre_kernel, sample=False, tm=tm),
        out_shape=(
            jax.ShapeDtypeStruct((b, N_HEADS, s, HEAD_PAD), BF16),
            jax.ShapeDtypeStruct((b, N_HEADS, s, HEAD_PAD), BF16),
            jax.ShapeDtypeStruct((b, N_HEADS, s, V_HEAD), BF16),
            jax.ShapeDtypeStruct((b, s, KV_LORA), F32),
            jax.ShapeDtypeStruct((b, s, QK_ROPE), F32),
            jax.ShapeDtypeStruct((b, s, pw), BF16),
            jax.ShapeDtypeStruct((b, HIST_PAD, pw), F32),
        ),
        grid=(b, s // tm),
        in_specs=[tok(d), mod(0), mod(1)] + [_const_spec(a) for a in consts_a] + [rope, rope]
        + [_const_spec(a) for a in consts_b],
        out_specs=(head(HEAD_PAD), head(HEAD_PAD), head(V_HEAD), tok(KV_LORA), tok(QK_ROPE), tok(pw),
                   pl.BlockSpec((None, HIST_PAD, pw), lambda i, j: (i, 0, 0))),
        scratch_shapes=[pltpu.VMEM((tm + HIST_PAD, pw), F32)]
        + [pltpu.VMEM((tm + HIST_PAD, pw // len(POOL_WINDOWS)), F32)] * 2,
        compiler_params=_params(("arbitrary", "arbitrary")),
        name="pre_prompt",
    )(x, mod_p, mod_p, *consts_a, cosx, sinx, *consts_b)


def _pre_sample(x, mod_s, cosx, sinx, w, bd):
    n, d = x.shape
    pw = w["pool_scale"].shape[1]
    consts = [w["norm1"], w["w_in"], w["q_a_norm"], w["w_q"], w["kv_a_norm"]]
    mod = lambda col: pl.BlockSpec((bd, d), lambda i: (0, col))
    full = lambda *shape: pl.BlockSpec(shape, lambda i: (0,) * len(shape))
    return pl.pallas_call(
        functools.partial(_pre_kernel, sample=True, tm=n),
        out_shape=(
            jax.ShapeDtypeStruct((N_HEADS, n, KV_LORA), BF16),
            jax.ShapeDtypeStruct((N_HEADS, n, LANES), BF16),
            jax.ShapeDtypeStruct((n, KV_LORA), F32),
            jax.ShapeDtypeStruct((n, QK_ROPE), F32),
            jax.ShapeDtypeStruct((n, pw), F32),
        ),
        grid=(1,),
        in_specs=[full(n, d), mod(0), mod(1)] + [_const_spec(a) for a in consts]
        + [full(n, LANES), full(n, LANES), _const_spec(w["q_gain"]), _const_spec(w["k_gain"]), _const_spec(w["w_kt"])],
        out_specs=(full(N_HEADS, n, KV_LORA), full(N_HEADS, n, LANES), full(n, KV_LORA), full(n, QK_ROPE),
                   full(n, pw)),
        compiler_params=_params(("arbitrary",)),
        name="pre_sample",
    )(x, mod_s, mod_s, *consts, cosx, sinx, w["q_gain"], w["k_gain"], w["w_kt"])


def _attn_kernel(q_ref, k_ref, v_ref, o_ref, *, seq, tile):
    n = seq // tile
    row = lax.broadcasted_iota(jnp.int32, (tile, tile), 0)
    col = lax.broadcasted_iota(jnp.int32, (tile, tile), 1)
    for qi in range(n):
        q = q_ref[qi * tile:(qi + 1) * tile, :]
        m = jnp.full((tile, 1), -jnp.inf, F32)
        l = jnp.zeros((tile, 1), F32)
        acc = jnp.zeros((tile, V_HEAD), F32)
        for ki in range(qi + 1):
            k = k_ref[ki * tile:(ki + 1) * tile, :]
            s = lax.dot_general(q, k, _NT, preferred_element_type=F32)
            if ki == qi:
                s = jnp.where(col <= row, s, -jnp.inf)
            m_new = jnp.maximum(m, jnp.max(s, axis=-1, keepdims=True))
            alpha = jnp.exp(m - m_new)
            p = jnp.exp(s - m_new)
            l = alpha * l + jnp.sum(p, axis=-1, keepdims=True)
            acc = alpha * acc + jnp.dot(p.astype(BF16), v_ref[ki * tile:(ki + 1) * tile, :],
                                        preferred_element_type=F32)
            m = m_new
        o_ref[qi * tile:(qi + 1) * tile, :] = (acc / l).astype(BF16)


def _attention_prompt(q, k, v):
    b, h, s, _ = q.shape
    head = lambda width: pl.BlockSpec((None, None, s, width), lambda i, j: (i, j, 0, 0))
    return pl.pallas_call(
        functools.partial(_attn_kernel, seq=s, tile=min(ATTN_TILE, s)),
        out_shape=jax.ShapeDtypeStruct((b, s, h * V_HEAD), BF16),
        grid=(b, h),
        in_specs=[head(HEAD_PAD), head(HEAD_PAD), head(V_HEAD)],
        out_specs=pl.BlockSpec((None, s, V_HEAD), lambda i, j: (i, 0, j)),
        compiler_params=_params(("arbitrary", "arbitrary")),
        name="attention_prompt",
    )(q, k, v)


def _sattn_kernel(pt_ref, q_ref, qr_ref, latn_ref, krn_ref, wkt_ref, clat_hbm, ckr_hbm, ctx_ref,
                  lat_buf, kr_buf, sem, lhs, s_even, s_odd, *, layer, n_seq, n_pages, page, n_new):
    b = pl.program_id(0)
    n_chunks = n_pages // CHUNK_PAGES
    per_stream = (n_seq // SEQ_STREAMS) * n_chunks
    g0 = b * n_chunks
    streams = range(SEQ_STREAMS)
    nq = q_ref.shape[1]
    n_k = wkt_ref.shape[0]
    s_bufs = (s_even, s_odd)

    def slot_of(g):
        return lax.rem(g, CACHE_SLOTS)

    def copies(st, g):
        slot = slot_of(g)
        out = []
        for p in range(CHUNK_PAGES):
            phys = pt_ref[(st * per_stream + g) * CHUNK_PAGES + p]
            dst = pl.ds(p * page, page)
            out.append(pltpu.make_async_copy(clat_hbm.at[layer, phys], lat_buf.at[st, slot, dst, :],
                                             sem.at[st, 0, slot]))
            out.append(pltpu.make_async_copy(ckr_hbm.at[layer, phys], kr_buf.at[st, slot, :, dst],
                                             sem.at[st, 1, slot]))
        return out

    def start(st, g):
        for cp in copies(st, g):
            cp.start()

    def wait(st, g):
        for cp in copies(st, g):
            cp.wait()

    def fetch_ahead(st, g):
        @pl.when(g + PREFETCH_CHUNKS < per_stream)
        def _():
            start(st, g + PREFETCH_CHUNKS)

    @pl.when(b == 0)
    def _():
        for st in streams:
            lhs[st, 0:n_k, :] = wkt_ref[...]
            for g in range(min(PREFETCH_CHUNKS, per_stream)):
                start(st, g)

    for st in streams:
        lhs[st, n_k:n_k + nq, :] = q_ref[st]

    def scores(st, lat, krt):
        t = lat.shape[0]
        big = lax.dot_general(lhs[st], lat.astype(BF16), _NT, preferred_element_type=F32)
        kn = big[0:n_k]
        ssq = jnp.sum((kn * kn).reshape(N_HEADS, QK_NOPE, t), axis=1)
        ssq = ssq + jnp.sum(krt * krt, axis=0, keepdims=True)
        r = lax.rsqrt(ssq * (1.0 / QK_HEAD) + EPS)
        sr = jnp.dot(qr_ref[st, :, 0:QK_ROPE], krt.astype(BF16), preferred_element_type=F32)
        return (big[n_k:] + sr) * jnp.tile(r, (nq // N_HEADS, 1))

    def update(s, lat, carry):
        m, l, acc = carry
        m_new = jnp.maximum(m, jnp.max(s, axis=-1, keepdims=True))
        alpha = jnp.exp(m - m_new)
        p = jnp.exp(s - m_new)
        l = alpha * l + jnp.sum(p, axis=-1, keepdims=True)
        acc = alpha * acc + jnp.dot(p.astype(BF16), lat.astype(BF16), preferred_element_type=F32)
        return m_new, l, acc

    for st in streams:
        wait(st, g0)
        fetch_ahead(st, g0)

    pad = LANES - latn_ref.shape[1]
    key = lax.broadcasted_iota(jnp.int32, (nq, LANES), 1)
    tok = lax.broadcasted_iota(jnp.int32, (nq, LANES), 0) // N_HEADS
    carries = []
    for st in streams:
        lat_new = jnp.concatenate([latn_ref[st], jnp.zeros((pad, KV_LORA), F32)], axis=0)
        s_new = jnp.where((key <= tok) & (key < n_new), scores(st, lat_new, krn_ref[st]), -jnp.inf)
        carry = (jnp.full((nq, 1), -jnp.inf, F32), jnp.zeros((nq, 1), F32), jnp.zeros((nq, KV_LORA), F32))
        carries.append(update(s_new, lat_new, carry))

    for st in streams:
        s_even[st] = scores(st, lat_buf[st, slot_of(g0)], kr_buf[st, slot_of(g0)])

    def enter(c):
        for st in streams:
            wait(st, g0 + c)
            fetch_ahead(st, g0 + c)

    def step(c, parity, carries):
        g = g0 + c
        out = []
        for st in streams:
            s_bufs[parity][st] = scores(st, lat_buf[st, slot_of(g)], kr_buf[st, slot_of(g)])
            out.append(update(s_bufs[1 - parity][st], lat_buf[st, slot_of(g - 1)], carries[st]))
        return tuple(out)

    def two_steps(i, carries):
        c = 1 + 2 * i
        enter(c)
        enter(c + 1)
        return step(c + 1, 0, step(c, 1, carries))

    carries = lax.fori_loop(0, (n_chunks - 1) // 2, two_steps, tuple(carries))
    if n_chunks % 2 == 0:
        enter(n_chunks - 1)
        carries = step(n_chunks - 1, 1, carries)
    for st in streams:
        m, l, acc = update(s_bufs[(n_chunks - 1) % 2][st], lat_buf[st, slot_of(g0 + n_chunks - 1)], carries[st])
        ctx_ref[st] = acc / l


def _attention_sample(page_table, qabs, qr, lat_new, krt_new, w_kt, cache_lat, cache_krt, layer, n_new):
    bd, nq, _ = qabs.shape
    n_pages = page_table.shape[1]
    page = cache_lat.shape[2]
    assert n_pages % CHUNK_PAGES == 0 and n_pages // CHUNK_PAGES >= 2 and bd % SEQ_STREAMS == 0
    rows = CHUNK_PAGES * page
    n_k = w_kt.shape[0]
    pt = page_table.reshape(bd // SEQ_STREAMS, SEQ_STREAMS, n_pages).transpose(1, 0, 2).reshape(-1)
    per_step = lambda r, c: pl.BlockSpec((SEQ_STREAMS, r, c), lambda i, pt: (i, 0, 0))
    grid_spec = pltpu.PrefetchScalarGridSpec(
        num_scalar_prefetch=1,
        grid=(bd // SEQ_STREAMS,),
        in_specs=[
            per_step(nq, KV_LORA), per_step(nq, LANES), per_step(SUBLANES, KV_LORA), per_step(QK_ROPE, LANES),
            pl.BlockSpec((n_k, KV_LORA), lambda i, pt: (0, 0)),
            pl.BlockSpec(memory_space=pl.ANY), pl.BlockSpec(memory_space=pl.ANY),
        ],
        out_specs=per_step(nq, KV_LORA),
        scratch_shapes=[
            pltpu.VMEM((SEQ_STREAMS, CACHE_SLOTS, rows, KV_LORA), F32),
            pltpu.VMEM((SEQ_STREAMS, CACHE_SLOTS, QK_ROPE, rows), F32),
            pltpu.SemaphoreType.DMA((SEQ_STREAMS, 2, CACHE_SLOTS)),
            pltpu.VMEM((SEQ_STREAMS, n_k + nq, KV_LORA), BF16),
            pltpu.VMEM((SEQ_STREAMS, nq, rows), F32),
            pltpu.VMEM((SEQ_STREAMS, nq, rows), F32),
        ],
    )
    return pl.pallas_call(
        functools.partial(_sattn_kernel, layer=layer, n_seq=bd, n_pages=n_pages, page=page, n_new=n_new),
        out_shape=jax.ShapeDtypeStruct((bd, nq, KV_LORA), F32),
        grid_spec=grid_spec,
        compiler_params=_params(("arbitrary",)),
        name="attention_sample",
    )(pt, qabs, qr, lat_new, krt_new, w_kt, cache_lat, cache_krt)


def _ctx_to_heads_kernel(ctx_ref, wv_ref, o_ref):
    n_tok, n_head, bd, _ = ctx_ref.shape
    for t in range(n_tok):
        for hd in range(n_head):
            o = jnp.dot(ctx_ref[t, hd].astype(BF16), wv_ref[hd], preferred_element_type=F32)
            o_ref[t * bd:(t + 1) * bd, hd * V_HEAD:(hd + 1) * V_HEAD] = o.astype(BF16)


def _ctx_to_heads(ctx, w_v):
    n_tok, n_head, bd, _ = ctx.shape
    return pl.pallas_call(
        _ctx_to_heads_kernel,
        out_shape=jax.ShapeDtypeStruct((n_tok * bd, n_head * V_HEAD), BF16),
        grid=(1,),
        in_specs=[_const_spec(ctx), _const_spec(w_v)],
        out_specs=pl.BlockSpec((n_tok * bd, n_head * V_HEAD), lambda i: (0, 0)),
        compiler_params=_params(("arbitrary",)),
        name="ctx_to_heads",
    )(ctx, w_v)


def _pool_sample_kernel(u_ref, hist_ref, wpool_ref, bpool_ref, pscale_ref, o_ref, *, first_pos):
    n_tok, bd, pw = u_ref.shape
    n_hist = hist_ref.shape[0]
    pg = pw // len(POOL_WINDOWS)

    def row(i, c0):
        return hist_ref[i, :, c0:c0 + pg] if i < n_hist else u_ref[i - n_hist, :, c0:c0 + pg]

    for t in range(n_tok):
        for g, w in enumerate(POOL_WINDOWS):
            c0 = g * pg
            ug = u_ref[t, :, c0:c0 + pg]
            acc = ug
            for j in range(1, w):
                acc = acc + row(n_hist + t - j, c0)
            d = acc / float(min(first_pos + t + 1, w)) - ug
            o = jnp.dot(d.astype(BF16), wpool_ref[g], preferred_element_type=F32) + bpool_ref[:, c0:c0 + pg]
            o_ref[t * bd:(t + 1) * bd, c0:c0 + pg] = (o * pscale_ref[:, c0:c0 + pg]).astype(BF16)


def _pool_sample(u, hist, w, first_pos):
    n_tok, bd, pw = u.shape
    args = (u, hist, w["w_pool"], w["b_pool"], w["pool_scale"])
    return pl.pallas_call(
        functools.partial(_pool_sample_kernel, first_pos=first_pos),
        out_shape=jax.ShapeDtypeStruct((n_tok * bd, pw), BF16),
        grid=(1,),
        in_specs=[_const_spec(a) for a in args],
        out_specs=pl.BlockSpec((n_tok * bd, pw), lambda i: (0, 0)),
        compiler_params=_params(("arbitrary",)),
        name="pool_sample",
    )(*args)


def _mix_kernel(x_ref, a_ref, p_ref, wout_ref, g1_ref, sh2_ref, sc2_ref, norm2_ref, x1_ref, h2_ref):
    tm = x_ref.shape[0]
    mixed = jnp.concatenate([a_ref[...], p_ref[...]], axis=1)
    mix = jnp.dot(mixed, wout_ref[...], preferred_element_type=F32)
    x1 = x_ref[...] + _rows(g1_ref[...], tm) * mix
    x1_ref[...] = x1
    h2 = _rms(x1, norm2_ref[...]) * (1.0 + _rows(sc2_ref[...], tm)) + _rows(sh2_ref[...], tm)
    h2_ref[...] = h2.astype(BF16)


def _ffn_kernel(h_ref, x1_ref, g2_ref, wg_hbm, wu_hbm, wd_hbm, y_ref, wg_buf, wu_buf, wd_buf, sem, *, n_steps):
    tf = wd_buf.shape[1]
    n_ff = wd_hbm.shape[0] // tf
    step = pl.program_id(0)
    g0 = step * n_ff
    g2 = _rows(g2_ref[...], y_ref.shape[0])

    def copies(j):
        slot = lax.rem(g0 + j, FFN_WEIGHT_SLOTS)
        cols = pl.ds((j % n_ff) * tf, tf)
        return (pltpu.make_async_copy(wg_hbm.at[:, cols], wg_buf.at[slot], sem.at[0, slot]),
                pltpu.make_async_copy(wu_hbm.at[:, cols], wu_buf.at[slot], sem.at[1, slot]),
                pltpu.make_async_copy(wd_hbm.at[cols, :], wd_buf.at[slot], sem.at[2, slot]))

    def start(j):
        for cp in copies(j):
            cp.start()

    def wait(j):
        for cp in copies(j):
            cp.wait()

    def if_more_steps(fn, j):
        if j < n_ff:
            fn(j)
        else:
            @pl.when(step + 1 < n_steps)
            def _():
                fn(j)

    @pl.when(step == 0)
    def _():
        for j in range(FFN_WEIGHT_SLOTS - 1):
            start(j)
        wait(0)

    h = h_ref[...]
    for j in range(n_ff):
        slot = lax.rem(g0 + j, FFN_WEIGHT_SLOTS)
        gate = jnp.dot(h, wg_buf[slot], preferred_element_type=F32)
        up = jnp.dot(h, wu_buf[slot], preferred_element_type=F32)
        act = (gate * jax.nn.sigmoid(gate) * up).astype(BF16)
        if_more_steps(wait, j + 1)
        if_more_steps(start, j + FFN_WEIGHT_SLOTS - 1)
        part = jnp.dot(act, wd_buf[slot], preferred_element_type=F32)
        y_ref[...] = (x1_ref[...] if j == 0 else y_ref[...]) + g2 * part


def _layer_out_prompt(x, attn_o, pool_o, mod_p, w):
    b, s, d = x.shape
    tm = PROMPT_TILE
    tok = lambda width: pl.BlockSpec((None, tm, width), lambda i, j: (i, j, 0))
    mod = lambda col: pl.BlockSpec((None, 1, d), lambda i, j: (i, 0, col))
    x1, h2 = pl.pallas_call(
        _mix_kernel,
        out_shape=(jax.ShapeDtypeStruct((b, s, d), F32), jax.ShapeDtypeStruct((b, s, d), BF16)),
        grid=(b, s // tm),
        in_specs=[tok(d), tok(attn_o.shape[2]), tok(pool_o.shape[2]), _const_spec(w["w_out"]),
                  mod(2), mod(3), mod(4), _const_spec(w["norm2"])],
        out_specs=(tok(d), tok(d)),
        compiler_params=_params(("arbitrary", "arbitrary")),
        name="mix_prompt",
    )(x, attn_o, pool_o, w["w_out"], mod_p, mod_p, mod_p, w["norm2"])

    tm = FFN_TOKEN_TILE
    nst = s // tm
    tok = pl.BlockSpec((None, tm, d), lambda i: (i // nst, i % nst, 0))
    return _ffn(h2, x1, mod_p, w, grid_steps=b * nst, tok_spec=tok,
                g2_spec=pl.BlockSpec((None, 1, d), lambda i: (i // nst, 0, 5)), name="ffn_prompt")


def _ffn(h2, x1, mod, w, *, grid_steps, tok_spec, g2_spec, name):
    d = h2.shape[-1]
    tf = FFN_HIDDEN_TILE
    assert w["w_gate"].shape[1] % tf == 0
    hbm = pl.BlockSpec(memory_space=pl.ANY)
    return pl.pallas_call(
        functools.partial(_ffn_kernel, n_steps=grid_steps),
        out_shape=jax.ShapeDtypeStruct(x1.shape, F32),
        grid=(grid_steps,),
        in_specs=[tok_spec, tok_spec, g2_spec, hbm, hbm, hbm],
        out_specs=tok_spec,
        scratch_shapes=[pltpu.VMEM((FFN_WEIGHT_SLOTS, d, tf), BF16), pltpu.VMEM((FFN_WEIGHT_SLOTS, d, tf), BF16),
                        pltpu.VMEM((FFN_WEIGHT_SLOTS, tf, d), BF16),
                        pltpu.SemaphoreType.DMA((3, FFN_WEIGHT_SLOTS))],
        compiler_params=_params(("arbitrary",)),
        name=name,
    )(h2, x1, mod, w["w_gate"], w["w_up"], w["w_down"])


def _layer_out_sample(x, attn_o, pool_o, mod_s, w, bd):
    n, d = x.shape
    full = lambda *shape: pl.BlockSpec(shape, lambda *_: (0,) * len(shape))
    mod = lambda col: pl.BlockSpec((bd, d), lambda *_: (0, col))
    x1, h2 = pl.pallas_call(
        _mix_kernel,
        out_shape=(jax.ShapeDtypeStruct((n, d), F32), jax.ShapeDtypeStruct((n, d), BF16)),
        grid=(1,),
        in_specs=[full(n, d), full(*attn_o.shape), full(*pool_o.shape), _const_spec(w["w_out"]),
                  mod(2), mod(3), mod(4), _const_spec(w["norm2"])],
        out_specs=(full(n, d), full(n, d)),
        compiler_params=_params(("arbitrary",)),
        name="mix_sample",
    )(x, attn_o, pool_o, w["w_out"], mod_s, mod_s, mod_s, w["norm2"])

    return _ffn(h2, x1, mod_s, w, grid_steps=1, tok_spec=full(n, d), g2_spec=mod(5), name="ffn_sample")


def _rope_tables(pos):
    inv = 1.0 / (ROPE_THETA ** (jnp.arange(0, QK_ROPE, 2, dtype=F32) / QK_ROPE))
    ang = pos.astype(F32)[:, None] * inv[None, :]
    c, s = jnp.cos(ang), jnp.sin(ang)
    z = jnp.zeros((pos.shape[0], LANES - QK_ROPE), F32)
    return jnp.concatenate([c, c, z], axis=1), jnp.concatenate([-s, s, z], axis=1)


def _layer_weights(l, w_in, q_a_norm, w_q_b, kv_a_norm, w_kv_b, q_norm_nope, q_norm_rope, k_norm_nope, k_norm_rope,
                   w_pool, b_pool, pool_scale, w_out, norm1, norm2, w_gate, w_up, w_down):
    half = QK_ROPE // 2
    o_kr = Q_LORA + KV_LORA
    o_u = o_kr + QK_ROPE
    win = w_in[l]
    w_in_p = jnp.concatenate([win[:, :o_u], win[:, o_kr + half:o_u], win[:, o_kr:o_kr + half], win[:, o_u:]], axis=1)
    wq = w_q_b[l].reshape(Q_LORA, N_HEADS, QK_HEAD)
    w_q_p = jnp.concatenate([wq, wq[:, :, QK_NOPE + half:], wq[:, :, QK_NOPE:QK_NOPE + half]], axis=2)
    wkv = w_kv_b[l].reshape(KV_LORA, N_HEADS, QK_NOPE + V_HEAD)
    zpad = jnp.zeros((HEAD_PAD - QK_HEAD,), F32)
    row = lambda a: a.reshape(1, -1)
    return {
        "norm1": row(norm1[l]), "norm2": row(norm2[l]),
        "w_in": w_in_p.astype(BF16),
        "q_a_norm": row(q_a_norm[l]), "kv_a_norm": row(kv_a_norm[l]),
        "w_q": w_q_p.reshape(Q_LORA, N_HEADS * HEAD_PAD).astype(BF16),
        "w_kv": w_kv_b[l].astype(BF16),
        "w_kt": wkv[:, :, :QK_NOPE].reshape(KV_LORA, N_HEADS * QK_NOPE).T.astype(BF16),
        "w_v": wkv[:, :, QK_NOPE:].transpose(1, 0, 2).astype(BF16),
        "q_gain": row(jnp.concatenate([q_norm_nope[l], q_norm_rope[l], q_norm_rope[l], zpad]) * SCALE),
        "k_gain": row(jnp.concatenate([k_norm_nope[l], k_norm_rope[l], k_norm_rope[l], zpad])),
        "w_pool": w_pool[l].astype(BF16), "b_pool": row(b_pool[l]), "pool_scale": row(pool_scale[l]),
        "w_out": w_out[l].astype(BF16),
        "w_gate": w_gate[l].astype(BF16), "w_up": w_up[l].astype(BF16), "w_down": w_down[l].astype(BF16),
    }


def kernel(x_prompt, x_sample, cache_kv_latent, cache_k_rope, state_pool, page_table, c_prompt, c_sample,
           w_mod, b_mod, norm1, w_in, q_a_norm, w_q_b, kv_a_norm, w_kv_b,
           q_norm_nope, q_norm_rope, k_norm_nope, k_norm_rope,
           w_pool, b_pool, pool_scale, w_out, norm2, w_gate, w_up, w_down):
    b, s, d = x_prompt.shape
    bd, t, _ = x_sample.shape
    assert s % PROMPT_TILE == 0 and s % FFN_TOKEN_TILE == 0 and PROMPT_TILE >= HIST_PAD
    assert t <= SUBLANES and bd % SUBLANES == 0
    depth = w_mod.shape[0]
    past = page_table.shape[1] * cache_kv_latent.shape[2]
    cos_p, sin_p = _rope_tables(jnp.arange(s))
    cos_s, sin_s = _rope_tables(jnp.repeat(past + jnp.arange(t), bd))
    c_all = jnp.concatenate([c_sample, c_prompt], axis=0)
    cache_krt = jnp.swapaxes(cache_k_rope, 2, 3)

    yp = x_prompt
    ys = x_sample.transpose(1, 0, 2).reshape(t * bd, d)
    outs = [[] for _ in range(6)]
    for l in range(depth):
        w = _layer_weights(l, w_in, q_a_norm, w_q_b, kv_a_norm, w_kv_b, q_norm_nope, q_norm_rope, k_norm_nope,
                           k_norm_rope, w_pool, b_pool, pool_scale, w_out, norm1, norm2, w_gate, w_up, w_down)
        mod = _modulation(c_all, w_mod[l], b_mod[l].reshape(1, -1))
        mod_p = mod[bd:].reshape(b, 1, -1)
        mod_s = mod

        q, k, v, lat_p, kr_p, pool_p, utail = _pre_prompt(yp, mod_p, cos_p, sin_p, w)
        attn_p = _attention_prompt(q, k, v)
        yp = _layer_out_prompt(yp, attn_p, pool_p, mod_p, w)

        qabs, qr, lat_s, kr_s, u_s = _pre_sample(ys, mod_s, cos_s, sin_s, w, bd)
        per_seq = lambda a: a.reshape(N_HEADS, t, bd, -1).transpose(2, 1, 0, 3).reshape(bd, t * N_HEADS, -1)
        lat_s = lat_s.reshape(t, bd, -1).transpose(1, 0, 2)
        kr_s = kr_s.reshape(t, bd, -1).transpose(1, 0, 2)
        lat_new = jnp.pad(lat_s, ((0, 0), (0, SUBLANES - t), (0, 0)))
        krt_new = jnp.pad(kr_s.transpose(0, 2, 1), ((0, 0), (0, 0), (0, LANES - t)))
        ctx = _attention_sample(page_table, per_seq(qabs), per_seq(qr), lat_new, krt_new, w["w_kt"],
                                cache_kv_latent, cache_krt, l, t)
        attn_s = _ctx_to_heads(ctx.reshape(bd, t, N_HEADS, -1).transpose(1, 2, 0, 3), w["w_v"])
        u_s = u_s.reshape(t, bd, -1)
        pool_s = _pool_sample(u_s, state_pool[l].transpose(1, 0, 2), w, past)
        ys = _layer_out_sample(ys, attn_s, pool_s, mod_s, w, bd)

        full_s = jnp.concatenate([state_pool[l], u_s.transpose(1, 0, 2)], axis=1)
        for dst, val in zip(outs, (lat_p, kr_p, utail[:, HIST_PAD - POOL_HIST:], lat_s, kr_s,
                                   full_s[:, -POOL_HIST:])):
            dst.append(val)

    ys = ys.reshape(t, bd, d).transpose(1, 0, 2)
    return (yp, ys) + tuple(jnp.stack(o) for o in outs)
```
